```python
import math
import jax, jax.numpy as jnp
from jax import lax
import numpy as np

D_MODEL = 4096
BATCH = 1
SEQ = 8192
DEPTH = 2

CHUNK = 64
N_MIXERS = 2
HGRN_EXPAND = 128
HGRN_HEADS = D_MODEL // HGRN_EXPAND
HGRN_HEAD_V = D_MODEL // HGRN_HEADS
SB_HEADS = 32
SB_HEAD_DIM = D_MODEL // SB_HEADS
Q_BLOCK = 128
D_FF = ((8 * D_MODEL // 3 + 255) // 256) * 256
N_HGRN_LAYERS = (DEPTH + 1) // 2
N_SB_LAYERS = DEPTH // 2
EPS = 1e-6

kernel_name = 'hybrid_hgrn2_stickbreaking_macaron'


def rmsnorm(x, w):
    x32 = x.astype(jnp.float32)
    y = x32 * lax.rsqrt(jnp.mean(x32 * x32, axis=-1, keepdims=True) + EPS)
    return (y * w.astype(jnp.float32)).astype(x.dtype)


def swiglu(u, w_gu, w_down):
    g, v = jnp.split(u @ w_gu, 2, axis=-1)
    return (jax.nn.silu(g) * v) @ w_down


def gla_chunk_scan(q, k, v, g):
    B, S, H, K = q.shape
    V = v.shape[-1]
    n_chunks = S // CHUNK

    def to_chunks(a):
        return a.reshape(B, n_chunks, CHUNK, H, a.shape[-1]).transpose(1, 0, 2, 3, 4)

    causal = jnp.tril(jnp.ones((CHUNK, CHUNK), dtype=bool))

    def step(state, inp):
        qc, kc, vc, gc = inp
        b = jnp.cumsum(gc, axis=1)
        o_inter = jnp.einsum('bthk,bhkv->bthv', qc * jnp.exp(b), state)
        diff = b[:, :, None] - b[:, None, :]
        decay = jnp.exp(jnp.where(causal[None, :, :, None, None], diff, -jnp.inf))
        attn = jnp.einsum('bthk,bshk,btshk->bhts', qc, kc, decay)
        o_intra = jnp.einsum('bhts,bshv->bthv', attn, vc)
        b_last = b[:, -1]
        k_dec = kc * jnp.exp(b_last[:, None] - b)
        new_state = jnp.exp(b_last)[..., None] * state + jnp.einsum('bshk,bshv->bhkv', k_dec, vc)
        return new_state, o_inter + o_intra

    state0 = jnp.zeros((B, H, K, V), jnp.float32)
    _, o = lax.scan(step, state0, (to_chunks(q), to_chunks(k), to_chunks(v), to_chunks(g)))
    return o.transpose(1, 0, 2, 3, 4).reshape(B, S, H, V)


def hgrn2_mixer(u, w_in, lb, g_norm, w_out):
    B, S, _ = u.shape
    q, f, i_in, gate = jnp.split(u @ w_in, 4, axis=-1)

    def heads(a):
        return a.reshape(B, S, HGRN_HEADS, -1).astype(jnp.float32)

    lb_h = lb.reshape(HGRN_HEADS, HGRN_EXPAND)
    q = jax.nn.silu(heads(q))
    f = lb_h + (1.0 - lb_h) * jax.nn.sigmoid(heads(f))
    o = gla_chunk_scan(q, 1.0 - f, heads(i_in), jnp.log(f))
    o = rmsnorm(o, g_norm) * jax.nn.silu(heads(gate))
    return o.reshape(B, S, D_MODEL).astype(u.dtype) @ w_out


def stick_breaking_mixer(u, w_in, w_out):
    B, S, _ = u.shape
    q, k, v = jnp.split(u @ w_in, 3, axis=-1)

    def heads(a):
        return a.reshape(B, S, SB_HEADS, SB_HEAD_DIM).transpose(0, 2, 1, 3).astype(jnp.float32)

    q, k, v = heads(q), heads(k), heads(v)
    n_blocks = S // Q_BLOCK
    q_blocks = q.reshape(B, SB_HEADS, n_blocks, Q_BLOCK, SB_HEAD_DIM).transpose(2, 0, 1, 3, 4)
    key_pos = jnp.arange(S)
    scale = 1.0 / math.sqrt(SB_HEAD_DIM)

    def block(args):
        q_blk, blk = args
        q_pos = blk * Q_BLOCK + jnp.arange(Q_BLOCK)
        strict = key_pos[None, :] < q_pos[:, None]
        z = jnp.einsum('bhqd,bhsd->bhqs', q_blk, k) * scale
        log_keep = jnp.where(strict, jax.nn.log_sigmoid(-z), 0.0)
        rev = lax.cumsum(log_keep, axis=3, reverse=True)
        log_survive = jnp.concatenate([rev[..., 1:], jnp.zeros_like(rev[..., :1])], axis=-1)
        a = jnp.where(strict, jnp.exp(jax.nn.log_sigmoid(z) + log_survive), 0.0)
        return jnp.einsum('bhqs,bhsd->bhqd', a, v)

    o = lax.map(block, (q_blocks, jnp.arange(n_blocks)))
    o = o.transpose(1, 0, 3, 2, 4).reshape(B, S, D_MODEL)
    return o.astype(u.dtype) @ w_out


def setup_inputs(seed: int = 0) -> dict:
    key = jax.random.key(seed)
    ks = jax.random.split(key, 16)
    nrm = jax.random.normal
    f32 = jnp.float32
    s_d = D_MODEL ** -0.5
    s_f = D_FF ** -0.5
    return {
        'x': nrm(ks[0], (BATCH, SEQ, D_MODEL), f32),
        'ffn1_norm': 1.0 + 0.01 * nrm(ks[1], (DEPTH, D_MODEL), f32),
        'ffn1_w_gu': nrm(ks[2], (DEPTH, D_MODEL, 2 * D_FF), f32) * s_d,
        'ffn1_w_down': nrm(ks[3], (DEPTH, D_FF, D_MODEL), f32) * s_f,
        'mix_norm': 1.0 + 0.01 * nrm(ks[4], (DEPTH, D_MODEL), f32),
        'ffn2_norm': 1.0 + 0.01 * nrm(ks[5], (DEPTH, D_MODEL), f32),
        'ffn2_w_gu': nrm(ks[6], (DEPTH, D_MODEL, 2 * D_FF), f32) * s_d,
        'ffn2_w_down': nrm(ks[7], (DEPTH, D_FF, D_MODEL), f32) * s_f,
        'hgrn_w_in': nrm(ks[8], (N_HGRN_LAYERS, D_MODEL, 4 * D_MODEL), f32) * s_d,
        'hgrn_lb_logits': 0.1 * nrm(ks[9], (DEPTH + 1, D_MODEL), f32),
        'hgrn_gnorm': 1.0 + 0.01 * nrm(ks[10], (N_HGRN_LAYERS, HGRN_HEAD_V), f32),
        'hgrn_w_out': nrm(ks[11], (N_HGRN_LAYERS, D_MODEL, D_MODEL), f32) * s_d,
        'sb_w_in': nrm(ks[12], (N_SB_LAYERS, D_MODEL, 3 * D_MODEL), f32) * s_d,
        'sb_w_out': nrm(ks[13], (N_SB_LAYERS, D_MODEL, D_MODEL), f32) * s_d,
        'final_norm': 1.0 + 0.01 * nrm(ks[14], (D_MODEL,), f32),
    }


def reference(x, ffn1_norm, ffn1_w_gu, ffn1_w_down, mix_norm, ffn2_norm, ffn2_w_gu, ffn2_w_down,
              hgrn_w_in, hgrn_lb_logits, hgrn_gnorm, hgrn_w_out, sb_w_in, sb_w_out, final_norm):
    lb_table = jnp.cumsum(jax.nn.softmax(hgrn_lb_logits.astype(jnp.float32), axis=0), axis=0)
    h = x
    for i in range(DEPTH):
        h = h + 0.5 * swiglu(rmsnorm(h, ffn1_norm[i]), ffn1_w_gu[i], ffn1_w_down[i])
        u = rmsnorm(h, mix_norm[i])
        j = i // N_MIXERS
        if i % N_MIXERS == 0:
            h = h + hgrn2_mixer(u, hgrn_w_in[j], lb_table[i], hgrn_gnorm[j], hgrn_w_out[j])
        else:
            h = h + stick_breaking_mixer(u, sb_w_in[j], sb_w_out[j])
        h = h + 0.5 * swiglu(rmsnorm(h, ffn2_norm[i]), ffn2_w_gu[i], ffn2_w_down[i])
    return rmsnorm(h, final_norm)
```

```python
import functools
import math

import jax
import jax.numpy as jnp
from jax import lax
from jax.experimental import pallas as pl
from jax.experimental.pallas import tpu as pltpu

EPS = 1e-6
CHUNK = 64
SUB = 16
HEAD_DIM = 128
FF_PAD = 1024

VMEM_LIMIT_BYTES = 56 * 1024 * 1024

BF16 = jnp.bfloat16
F32 = jnp.float32


def _params(*sem):
    return pltpu.CompilerParams(dimension_semantics=sem, vmem_limit_bytes=VMEM_LIMIT_BYTES)


def _dot(a, b):
    return jnp.dot(a, b, preferred_element_type=F32)


def _dot_nt(a, b):
    return lax.dot_general(a, b, (((1,), (1,)), ((), ())), preferred_element_type=F32)


def _dot_tn(a, b):
    return lax.dot_general(a, b, (((0,), (0,)), ((), ())), preferred_element_type=F32)


def _sigmoid(x):
    return 1.0 / (1.0 + jnp.exp(-x))


def _silu(x):
    return x * _sigmoid(x)


def _rmsnorm_kernel(x_ref, w_ref, o_ref):
    x = x_ref[...]
    y = x * lax.rsqrt(jnp.mean(x * x, axis=-1, keepdims=True) + EPS)
    o_ref[...] = (y * w_ref[...]).astype(o_ref.dtype)


def rmsnorm(x, w, out_dtype, tm=256):
    s, d = x.shape
    return pl.pallas_call(
        _rmsnorm_kernel,
        out_shape=jax.ShapeDtypeStruct((s, d), out_dtype),
        grid=(s // tm,),
        in_specs=[pl.BlockSpec((tm, d), lambda i: (i, 0)),
                  pl.BlockSpec((1, d), lambda i: (0, 0))],
        out_specs=pl.BlockSpec((tm, d), lambda i: (i, 0)),
        compiler_params=_params("parallel"),
        name="rmsnorm",
    )(x, w.reshape(1, d))


def _matmul_kernel(x_ref, w_ref, o_ref):
    o_ref[...] = _dot(x_ref[...], w_ref[...]).astype(o_ref.dtype)


def matmul(x, w, out_dtype, tm=1024, tn=1024):
    m, k = x.shape
    n = w.shape[1]
    return pl.pallas_call(
        _matmul_kernel,
        out_shape=jax.ShapeDtypeStruct((m, n), out_dtype),
        grid=(m // tm, n // tn),
        in_specs=[pl.BlockSpec((tm, k), lambda i, j: (i, 0)),
                  pl.BlockSpec((k, tn), lambda i, j: (0, j))],
        out_specs=pl.BlockSpec((tm, tn), lambda i, j: (i, j)),
        compiler_params=_params("parallel", "parallel"),
        name="matmul",
    )(x, w)


def _matmul_residual_kernel(x_ref, w_ref, h_ref, o_ref, *, scale, nk):
    kk = pl.program_id(2)
    part = _dot(x_ref[...], w_ref[...])

    @pl.when(kk == 0)
    def _():
        o_ref[...] = part

    @pl.when(kk > 0)
    def _():
        o_ref[...] += part

    @pl.when(kk == nk - 1)
    def _():
        o_ref[...] = h_ref[...] + scale * o_ref[...]


def matmul_residual(x, w, h, scale, tm=1024, tn=1024, tk=None):
    m, k = x.shape
    n = w.shape[1]
    tk = k if tk is None else tk
    nk = k // tk
    return pl.pallas_call(
        functools.partial(_matmul_residual_kernel, scale=scale, nk=nk),
        out_shape=jax.ShapeDtypeStruct((m, n), F32),
        grid=(m // tm, n // tn, nk),
        in_specs=[pl.BlockSpec((tm, tk), lambda i, j, kk: (i, kk)),
                  pl.BlockSpec((tk, tn), lambda i, j, kk: (kk, j)),
                  pl.BlockSpec((tm, tn), lambda i, j, kk: (i, j))],
        out_specs=pl.BlockSpec((tm, tn), lambda i, j, kk: (i, j)),
        compiler_params=_params("parallel", "parallel", "arbitrary"),
        name="matmul_residual",
    )(x, w, h)


def _ffn_up_kernel(x_ref, wg_ref, wu_ref, o_ref):
    x = x_ref[...]
    g = _dot(x, wg_ref[...])
    u = _dot(x, wu_ref[...])
    o_ref[...] = (_silu(g) * u).astype(o_ref.dtype)


def ffn_up(x, w_gu, tm=1024, tn=512):
    m, k = x.shape
    f = w_gu.shape[1] // 2
    nj = f // tn
    return pl.pallas_call(
        _ffn_up_kernel,
        out_shape=jax.ShapeDtypeStruct((m, f), BF16),
        grid=(m // tm, nj),
        in_specs=[pl.BlockSpec((tm, k), lambda i, j: (i, 0)),
                  pl.BlockSpec((k, tn), lambda i, j: (0, j)),
                  pl.BlockSpec((k, tn), lambda i, j: (0, j + nj))],
        out_specs=pl.BlockSpec((tm, tn), lambda i, j: (i, j)),
        compiler_params=_params("parallel", "parallel"),
        name="ffn_up",
    )(x, w_gu, w_gu)


def _split3(x):
    a = x.astype(BF16)
    r = x - a.astype(F32)
    b = r.astype(BF16)
    c = (r - b.astype(F32)).astype(BF16)
    return a, b, c


def _hgrn_kernel(q_ref, f_ref, i_ref, gate_ref, lbl_ref, gn_ref, u_ref, o_ref,
                 q_s, k_s, b_s, p_s, st_s, d_s, *, layer, tc):
    n_chunks = tc // CHUNK
    n_sub = CHUNK // SUB

    @pl.when(pl.program_id(1) == 0)
    def _():
        st_s[...] = jnp.zeros_like(st_s)

    lbl = lbl_ref[...]
    e = jnp.exp(lbl - jnp.max(lbl, axis=0, keepdims=True))
    lb = jnp.sum(e[:layer + 1], axis=0, keepdims=True) / jnp.sum(e, axis=0, keepdims=True)

    row = lax.broadcasted_iota(jnp.int32, (CHUNK, CHUNK), 0)
    col = lax.broadcasted_iota(jnp.int32, (CHUNK, CHUNK), 1)
    tril = jnp.where(col <= row, 1.0, 0.0).astype(BF16)
    tril3 = jnp.concatenate([tril, tril, tril], axis=1)
    diag_mask = ((row // SUB) == (col // SUB)) & (col <= row)
    off_mask = (col // SUB) < (row // SUB)

    def phase1(c, carry):
        r0 = pl.multiple_of(c * CHUNK, CHUNK)
        rows = pl.ds(r0, CHUNK)
        q = _silu(q_ref[rows, :])
        f = lb + (1.0 - lb) * _sigmoid(f_ref[rows, :])
        k = 1.0 - f
        g = jnp.log(f)
        g1, g2, g3 = _split3(g)
        b = _dot(tril3, jnp.concatenate([g1, g2, g3], axis=0))
        q_s[rows, :] = q
        k_s[rows, :] = k
        b_s[rows, :] = b
        for i in range(n_sub):
            qi = q[i * SUB:(i + 1) * SUB]
            bi = b[i * SUB:(i + 1) * SUB]
            for s in range(SUB):
                r = i * SUB + s
                p = qi * k[r:r + 1] * jnp.exp(jnp.minimum(bi - b[r:r + 1], 0.0))
                p_s[pl.ds(r0 + i * SUB, SUB), s * HEAD_DIM:(s + 1) * HEAD_DIM] = p.astype(BF16)
        return carry

    lax.fori_loop(0, n_chunks, phase1, 0)

    d_s[...] = _dot(p_s[...], u_ref[...])

    gn = gn_ref[...]

    def phase3(c, carry):
        r0 = pl.multiple_of(c * CHUNK, CHUNK)
        rows = pl.ds(r0, CHUNK)
        q = q_s[rows, :]
        k = k_s[rows, :]
        b = b_s[rows, :]
        vb = i_ref[rows, :].astype(BF16)
        st = st_s[...]

        o = _dot_nt((q * jnp.exp(b)).astype(BF16), st.astype(BF16))

        blocks = [jnp.zeros((SUB, CHUNK), F32)]
        for i in range(1, n_sub):
            bref = b[i * SUB - 1:i * SUB]
            qs = q[i * SUB:(i + 1) * SUB] * jnp.exp(b[i * SUB:(i + 1) * SUB] - bref)
            ks = k * jnp.exp(jnp.minimum(bref - b, 0.0))
            blocks.append(_dot_nt(qs.astype(BF16), ks.astype(BF16)))
        a_off = jnp.concatenate(blocks, axis=0)
        d = d_s[rows, :][:, :CHUNK]
        a = jnp.where(diag_mask, d, jnp.where(off_mask, a_off, 0.0))
        o = o + _dot(a.astype(BF16), vb)

        b_last = b[CHUNK - 1:CHUNK]
        k_dec = k * jnp.exp(b_last - b)
        st_s[...] = st * jnp.exp(b_last) + _dot_tn(vb, k_dec.astype(BF16))

        y = o * lax.rsqrt(jnp.mean(o * o, axis=-1, keepdims=True) + EPS) * gn
        o_ref[rows, :] = (y * _silu(gate_ref[rows, :])).astype(o_ref.dtype)
        return carry

    lax.fori_loop(0, n_chunks, phase3, 0)


def hgrn_core(proj, lb_logits, gnorm, layer, tc=512):
    s, d4 = proj.shape
    d = d4 // 4
    heads = d // HEAD_DIM
    n_l = lb_logits.shape[0]
    r = jnp.arange(SUB * HEAD_DIM)[:, None] // HEAD_DIM
    c = jnp.arange(HEAD_DIM)[None, :] % SUB
    u = (r == c).astype(BF16)

    def col(off):
        return pl.BlockSpec((tc, HEAD_DIM), lambda h, t: (t, off * heads + h))

    return pl.pallas_call(
        functools.partial(_hgrn_kernel, layer=layer, tc=tc),
        out_shape=jax.ShapeDtypeStruct((s, d), BF16),
        grid=(heads, s // tc),
        in_specs=[col(0), col(1), col(2), col(3),
                  pl.BlockSpec((n_l, HEAD_DIM), lambda h, t: (0, h)),
                  pl.BlockSpec((1, HEAD_DIM), lambda h, t: (0, 0)),
                  pl.BlockSpec((SUB * HEAD_DIM, HEAD_DIM), lambda h, t: (0, 0))],
        out_specs=pl.BlockSpec((tc, HEAD_DIM), lambda h, t: (t, h)),
        scratch_shapes=[pltpu.VMEM((tc, HEAD_DIM), F32),
                        pltpu.VMEM((tc, HEAD_DIM), F32),
                        pltpu.VMEM((tc, HEAD_DIM), F32),
                        pltpu.VMEM((tc, SUB * HEAD_DIM), BF16),
                        pltpu.VMEM((HEAD_DIM, HEAD_DIM), F32),
                        pltpu.VMEM((tc, HEAD_DIM), F32)],
        compiler_params=_params("parallel", "arbitrary"),
        name="hgrn_core",
    )(proj, proj, proj, proj, lb_logits, gnorm.reshape(1, HEAD_DIM), u)


def _sb_kernel(q_ref, k_ref, v_ref, w_ref, o_ref, acc_s, c_s, *, tq, tk, scale):
    qi = pl.program_id(1)
    q = q_ref[...]
    w = w_ref[...]
    acc_s[...] = jnp.zeros_like(acc_s)
    c_s[...] = jnp.zeros_like(c_s)
    n_diag = tq // tk

    def tile(k0, mask):
        kb = k_ref[pl.ds(k0, tk), :]
        vb = v_ref[pl.ds(k0, tk), :]
        z = _dot_nt(q, kb) * scale
        lk = -(jnp.maximum(z, 0.0) + jnp.log(1.0 + jnp.exp(-jnp.abs(z))))
        if mask is not None:
            lk = jnp.where(mask, lk, 0.0)
        hi = lk.astype(BF16)
        lo = (lk - hi.astype(F32)).astype(BF16)
        r = _dot(jnp.concatenate([hi, lo], axis=1), w)
        c = c_s[...]
        cc = jnp.concatenate([c] * (tk // HEAD_DIM), axis=1)
        a = jnp.exp(z + r + cc)
        if mask is not None:
            a = jnp.where(mask, a, 0.0)
        acc_s[...] += _dot(a.astype(BF16), vb)
        c_s[...] = c + jnp.broadcast_to(r[:, 0:1], c.shape)

    row = lax.broadcasted_iota(jnp.int32, (tq, tk), 0)
    col = lax.broadcasted_iota(jnp.int32, (tq, tk), 1)
    q0 = pl.multiple_of(qi * tq, tq)
    for d in range(n_diag - 1, -1, -1):
        tile(q0 + d * tk, (col + d * tk) < row)

    n_full = qi * n_diag

    def body(j, carry):
        tile(pl.multiple_of((n_full - 1 - j) * tk, tk), None)
        return carry

    lax.fori_loop(0, n_full, body, 0)
    o_ref[...] = acc_s[...].astype(o_ref.dtype)


def sb_core(qkv, tq=512, tk=256):
    s, d3 = qkv.shape
    d = d3 // 3
    heads = d // HEAD_DIM
    j = jnp.arange(tk)
    tinc = (j[:, None] >= j[None, :]).astype(BF16)
    w = jnp.concatenate([tinc, tinc], axis=0)
    kern = functools.partial(_sb_kernel, tq=tq, tk=tk, scale=1.0 / math.sqrt(HEAD_DIM))
    return pl.pallas_call(
        kern,
        out_shape=jax.ShapeDtypeStruct((s, d), BF16),
        grid=(heads, s // tq),
        in_specs=[pl.BlockSpec((tq, HEAD_DIM), lambda h, i: (i, h)),
                  pl.BlockSpec((s, HEAD_DIM), lambda h, i: (0, heads + h)),
                  pl.BlockSpec((s, HEAD_DIM), lambda h, i: (0, 2 * heads + h)),
                  pl.BlockSpec((2 * tk, tk), lambda h, i: (0, 0))],
        out_specs=pl.BlockSpec((tq, HEAD_DIM), lambda h, i: (i, h)),
        scratch_shapes=[pltpu.VMEM((tq, HEAD_DIM), F32),
                        pltpu.VMEM((tq, HEAD_DIM), F32)],
        compiler_params=_params("parallel", "arbitrary"),
        name="sb_core",
    )(qkv, qkv, qkv, w)


def _pad_ff(w_gu, w_down):
    d, f2 = w_gu.shape
    f = f2 // 2
    fp = -(-f // FF_PAD) * FF_PAD
    gu = jnp.pad(w_gu.astype(BF16).reshape(d, 2, f), ((0, 0), (0, 0), (0, fp - f)))
    down = jnp.pad(w_down.astype(BF16), ((0, fp - f), (0, 0)))
    return gu.reshape(d, 2 * fp), down


def _ffn_half_step(h, norm_w, w_gu, w_down):
    gu, down = _pad_ff(w_gu, w_down)
    u = rmsnorm(h, norm_w, BF16)
    a = ffn_up(u, gu)
    return matmul_residual(a, down, h, 0.5, tk=a.shape[1] // 4)


def kernel(x, ffn1_norm, ffn1_w_gu, ffn1_w_down, mix_norm, ffn2_norm, ffn2_w_gu, ffn2_w_down,
           hgrn_w_in, hgrn_lb_logits, hgrn_gnorm, hgrn_w_out, sb_w_in, sb_w_out, final_norm):
    b, s, d = x.shape
    depth = ffn1_norm.shape[0]
    outs = []
    for bi in range(b):
        h = x[bi]
        for i in range(depth):
            h = _ffn_half_step(h, ffn1_norm[i], ffn1_w_gu[i], ffn1_w_down[i])
            u = rmsnorm(h, mix_norm[i], BF16)
            j = i // 2
            if i % 2 == 0:
                proj = matmul(u, hgrn_w_in[j].astype(BF16), F32)
                o = hgrn_core(proj, hgrn_lb_logits, hgrn_gnorm[j], layer=i)
                h = matmul_residual(o, hgrn_w_out[j].astype(BF16), h, 1.0)
            else:
                qkv = matmul(u, sb_w_in[j].astype(BF16), BF16)
                o = sb_core(qkv)
                h = matmul_residual(o, sb_w_out[j].astype(BF16), h, 1.0)
            h = _ffn_half_step(h, ffn2_norm[i], ffn2_w_gu[i], ffn2_w_down[i])
        outs.append(rmsnorm(h, final_norm, x.dtype))
    return jnp.stack(outs, axis=0)
```

```python
import functools
import math

import jax
import jax.numpy as jnp
from jax import lax
from jax.experimental import pallas as pl
from jax.experimental.pallas import tpu as pltpu

EPS = 1e-6
CHUNK = 64
SUB = 16
HEAD_DIM = 128
FF_PAD = 1024
FF_TILE = 256

VMEM_LIMIT_BYTES = 56 * 1024 * 1024

BF16 = jnp.bfloat16
F32 = jnp.float32


def _params(*sem):
    return pltpu.CompilerParams(dimension_semantics=sem, vmem_limit_bytes=VMEM_LIMIT_BYTES)


def _dot(a, b):
    return jnp.dot(a, b, preferred_element_type=F32)


def _dot_nt(a, b):
    return lax.dot_general(a, b, (((1,), (1,)), ((), ())), preferred_element_type=F32)


def _dot_tn(a, b):
    return lax.dot_general(a, b, (((0,), (0,)), ((), ())), preferred_element_type=F32)


def _sigmoid(x):
    return 1.0 / (1.0 + jnp.exp(-x))


def _silu(x):
    return x * _sigmoid(x)


def _rmsnorm_kernel(x_ref, w_ref, o_ref):
    x = x_ref[...]
    y = x * lax.rsqrt(jnp.mean(x * x, axis=-1, keepdims=True) + EPS)
    o_ref[...] = (y * w_ref[...]).astype(o_ref.dtype)


def rmsnorm(x, w, out_dtype, tm=256):
    s, d = x.shape
    return pl.pallas_call(
        _rmsnorm_kernel,
        out_shape=jax.ShapeDtypeStruct((s, d), out_dtype),
        grid=(s // tm,),
        in_specs=[pl.BlockSpec((tm, d), lambda i: (i, 0)),
                  pl.BlockSpec((1, d), lambda i: (0, 0))],
        out_specs=pl.BlockSpec((tm, d), lambda i: (i, 0)),
        compiler_params=_params("parallel"),
        name="rmsnorm",
    )(x, w.reshape(1, d))


def _cast_kernel(x_ref, o_ref, *, axis, n_valid):
    x = x_ref[...].astype(o_ref.dtype)
    if n_valid is not None:
        x = jnp.where(pl.program_id(axis) < n_valid, x, jnp.zeros_like(x))
    o_ref[...] = x


def cast_rows(w, layer, dtype, tr, rows_out=None):
    _, r, n = w.shape
    rows_out = r if rows_out is None else rows_out
    n_valid = r // tr
    return pl.pallas_call(
        functools.partial(_cast_kernel, axis=0, n_valid=None if rows_out == r else n_valid),
        out_shape=jax.ShapeDtypeStruct((rows_out, n), dtype),
        grid=(rows_out // tr,),
        in_specs=[pl.BlockSpec((None, tr, n), lambda i: (layer, jnp.minimum(i, n_valid - 1), 0))],
        out_specs=pl.BlockSpec((tr, n), lambda i: (i, 0)),
        compiler_params=_params("parallel"),
        name="cast_rows",
    )(w)


def cast_col_groups(w, layer, dtype, groups, tn, cols_out):
    _, r, gf = w.shape
    f = gf // groups
    n_valid = f // tn
    n_out = cols_out // tn
    return pl.pallas_call(
        functools.partial(_cast_kernel, axis=1, n_valid=n_valid),
        out_shape=jax.ShapeDtypeStruct((r, groups * cols_out), dtype),
        grid=(groups, n_out),
        in_specs=[pl.BlockSpec((None, r, tn),
                               lambda g, j: (layer, 0, g * n_valid + jnp.minimum(j, n_valid - 1)))],
        out_specs=pl.BlockSpec((r, tn), lambda g, j: (0, g * n_out + j)),
        compiler_params=_params("parallel", "parallel"),
        name="cast_col_groups",
    )(w)


def _matmul_kernel(x_ref, w_ref, o_ref, *, scaled_blocks, scale):
    acc = _dot(x_ref[...], w_ref[...])
    if scaled_blocks:
        acc = acc * jnp.where(pl.program_id(1) < scaled_blocks, scale, 1.0)
    o_ref[...] = acc.astype(o_ref.dtype)


def matmul(x, w, out_dtype, tm=1024, tn=1024, scaled_cols=0, scale=1.0):
    m, k = x.shape
    n = w.shape[1]
    return pl.pallas_call(
        functools.partial(_matmul_kernel, scaled_blocks=scaled_cols // tn, scale=scale),
        out_shape=jax.ShapeDtypeStruct((m, n), out_dtype),
        grid=(m // tm, n // tn),
        in_specs=[pl.BlockSpec((tm, k), lambda i, j: (i, 0)),
                  pl.BlockSpec((k, tn), lambda i, j: (0, j))],
        out_specs=pl.BlockSpec((tm, tn), lambda i, j: (i, j)),
        compiler_params=_params("parallel", "parallel"),
        name="matmul",
    )(x, w)


def _matmul_residual_kernel(x_ref, w_ref, h_ref, o_ref, *, scale, nk):
    kk = pl.program_id(2)
    part = _dot(x_ref[...], w_ref[...])

    @pl.when(kk == 0)
    def _():
        o_ref[...] = part

    @pl.when(kk > 0)
    def _():
        o_ref[...] += part

    @pl.when(kk == nk - 1)
    def _():
        o_ref[...] = h_ref[...] + scale * o_ref[...]


def matmul_residual(x, w, h, scale, tm=1024, tn=1024, tk=None):
    m, k = x.shape
    n = w.shape[1]
    tk = k if tk is None else tk
    nk = k // tk
    return pl.pallas_call(
        functools.partial(_matmul_residual_kernel, scale=scale, nk=nk),
        out_shape=jax.ShapeDtypeStruct((m, n), F32),
        grid=(m // tm, n // tn, nk),
        in_specs=[pl.BlockSpec((tm, tk), lambda i, j, kk: (i, kk)),
                  pl.BlockSpec((tk, tn), lambda i, j, kk: (kk, j)),
                  pl.BlockSpec((tm, tn), lambda i, j, kk: (i, j))],
        out_specs=pl.BlockSpec((tm, tn), lambda i, j, kk: (i, j)),
        compiler_params=_params("parallel", "parallel", "arbitrary"),
        name="matmul_residual",
    )(x, w, h)


def _ffn_up_kernel(x_ref, wg_ref, wu_ref, o_ref):
    x = x_ref[...]
    g = _dot(x, wg_ref[...])
    u = _dot(x, wu_ref[...])
    o_ref[...] = (_silu(g) * u).astype(o_ref.dtype)


def ffn_up(x, w_gu, tm=1024, tn=512):
    m, k = x.shape
    f = w_gu.shape[1] // 2
    nj = f // tn
    return pl.pallas_call(
        _ffn_up_kernel,
        out_shape=jax.ShapeDtypeStruct((m, f), BF16),
        grid=(m // tm, nj),
        in_specs=[pl.BlockSpec((tm, k), lambda i, j: (i, 0)),
                  pl.BlockSpec((k, tn), lambda i, j: (0, j)),
                  pl.BlockSpec((k, tn), lambda i, j: (0, j + nj))],
        out_specs=pl.BlockSpec((tm, tn), lambda i, j: (i, j)),
        compiler_params=_params("parallel", "parallel"),
        name="ffn_up",
    )(x, w_gu, w_gu)


def _split3(x):
    a = x.astype(BF16)
    r = x - a.astype(F32)
    b = r.astype(BF16)
    c = (r - b.astype(F32)).astype(BF16)
    return a, b, c


def _hgrn_kernel(q_ref, f_ref, i_ref, gate_ref, lbl_ref, gn_ref, u_ref, o_ref,
                 q_s, k_s, b_s, p_s, st_s, d_s, *, layer, tc):
    n_chunks = tc // CHUNK
    n_sub = CHUNK // SUB

    @pl.when(pl.program_id(1) == 0)
    def _():
        st_s[...] = jnp.zeros_like(st_s)

    lbl = lbl_ref[...]
    e = jnp.exp(lbl - jnp.max(lbl, axis=0, keepdims=True))
    lb = jnp.sum(e[:layer + 1], axis=0, keepdims=True) / jnp.sum(e, axis=0, keepdims=True)

    row = lax.broadcasted_iota(jnp.int32, (CHUNK, CHUNK), 0)
    col = lax.broadcasted_iota(jnp.int32, (CHUNK, CHUNK), 1)
    tril = jnp.where(col <= row, 1.0, 0.0).astype(BF16)
    tril3 = jnp.concatenate([tril, tril, tril], axis=1)
    diag_mask = ((row // SUB) == (col // SUB)) & (col <= row)
    off_mask = (col // SUB) < (row // SUB)


    for c in range(n_chunks):
        rows = slice(c * CHUNK, (c + 1) * CHUNK)
        q = _silu(q_ref[rows, :])
        f = lb + (1.0 - lb) * _sigmoid(f_ref[rows, :])
        k = 1.0 - f
        g1, g2, g3 = _split3(jnp.log2(f))
        b = _dot(tril3, jnp.concatenate([g1, g2, g3], axis=0))
        q_s[rows, :] = q
        k_s[rows, :] = k
        b_s[rows, :] = b
        for i in range(n_sub):
            r0 = c * CHUNK + i * SUB
            qi = q[i * SUB:(i + 1) * SUB]
            bi = b[i * SUB:(i + 1) * SUB]
            for s in range(SUB):
                r = i * SUB + s
                p = qi * k[r:r + 1] * jnp.exp2(jnp.minimum(bi - b[r:r + 1], 0.0))
                p_s[r0:r0 + SUB, s * HEAD_DIM:(s + 1) * HEAD_DIM] = p.astype(BF16)

    d_s[...] = _dot(p_s[...], u_ref[...])

    gn = gn_ref[...]
    st = st_s[...]
    for c in range(n_chunks):
        rows = slice(c * CHUNK, (c + 1) * CHUNK)
        q = q_s[rows, :]
        k = k_s[rows, :]
        b = b_s[rows, :]
        vb = i_ref[rows, :].astype(BF16)

        o = _dot_nt((q * jnp.exp2(b)).astype(BF16), st.astype(BF16))

        blocks = [jnp.zeros((SUB, CHUNK), F32)]
        for i in range(1, n_sub):
            bref = b[i * SUB - 1:i * SUB]
            qs = q[i * SUB:(i + 1) * SUB] * jnp.exp2(b[i * SUB:(i + 1) * SUB] - bref)
            ks = k * jnp.exp2(jnp.minimum(bref - b, 0.0))
            blocks.append(_dot_nt(qs.astype(BF16), ks.astype(BF16)))
        a_off = jnp.concatenate(blocks, axis=0)
        d = d_s[rows, :][:, :CHUNK]
        a = jnp.where(diag_mask, d, jnp.where(off_mask, a_off, 0.0))
        o = o + _dot(a.astype(BF16), vb)

        b_last = b[CHUNK - 1:CHUNK]
        k_dec = k * jnp.exp2(b_last - b)
        st = st * jnp.exp2(b_last) + _dot_tn(vb, k_dec.astype(BF16))

        y = o * lax.rsqrt(jnp.mean(o * o, axis=-1, keepdims=True) + EPS) * gn
        o_ref[rows, :] = (y * _silu(gate_ref[rows, :])).astype(o_ref.dtype)
    st_s[...] = st


def hgrn_core(proj, lb_logits, gnorm, layer, tc=512):
    s, d4 = proj.shape
    d = d4 // 4
    heads = d // HEAD_DIM
    n_l = lb_logits.shape[0]
    r = jnp.arange(SUB * HEAD_DIM)[:, None] // HEAD_DIM
    c = jnp.arange(HEAD_DIM)[None, :] % SUB
    u = (r == c).astype(BF16)

    def col(off):
        return pl.BlockSpec((tc, HEAD_DIM), lambda h, t: (t, off * heads + h))

    return pl.pallas_call(
        functools.partial(_hgrn_kernel, layer=layer, tc=tc),
        out_shape=jax.ShapeDtypeStruct((s, d), BF16),
        grid=(heads, s // tc),
        in_specs=[col(0), col(1), col(2), col(3),
                  pl.BlockSpec((n_l, HEAD_DIM), lambda h, t: (0, h)),
                  pl.BlockSpec((1, HEAD_DIM), lambda h, t: (0, 0)),
                  pl.BlockSpec((SUB * HEAD_DIM, HEAD_DIM), lambda h, t: (0, 0))],
        out_specs=pl.BlockSpec((tc, HEAD_DIM), lambda h, t: (t, h)),
        scratch_shapes=[pltpu.VMEM((tc, HEAD_DIM), F32),
                        pltpu.VMEM((tc, HEAD_DIM), F32),
                        pltpu.VMEM((tc, HEAD_DIM), F32),
                        pltpu.VMEM((tc, SUB * HEAD_DIM), BF16),
                        pltpu.VMEM((HEAD_DIM, HEAD_DIM), F32),
                        pltpu.VMEM((tc, HEAD_DIM), F32)],
        compiler_params=_params("parallel", "arbitrary"),
        name="hgrn_core",
    )(proj, proj, proj, proj, lb_logits, gnorm.reshape(1, HEAD_DIM), u)


def _sb_kernel(q_ref, k_ref, v_ref, w_ref, o_ref, acc_s, c_s, *, tq, sk, hpb):
    qi = pl.program_id(1)
    w = w_ref[...]
    acc_s[...] = jnp.zeros_like(acc_s)
    c_s[...] = jnp.zeros_like(c_s)
    n_sub = tq // sk

    def tile(k0, mask):
        for hh in range(hpb):
            hc = slice(hh * HEAD_DIM, (hh + 1) * HEAD_DIM)
            kb = k_ref[pl.ds(k0, tq), hc]
            vb = v_ref[pl.ds(k0, tq), hc]
            z = _dot_nt(q_ref[:, hc], kb)
            nz = -z
            lk = jnp.minimum(nz, 0.0) - jnp.log2(1.0 + jnp.exp2(jnp.minimum(z, nz)))
            if mask is not None:
                lk = jnp.where(mask, lk, 0.0)
            lkb = lk.astype(BF16)
            c = c_s[hh]
            parts = []
            for s in range(n_sub - 1, -1, -1):
                cols = slice(s * sk, (s + 1) * sk)
                r = _dot(lkb[:, cols], w)
                cc = jnp.concatenate([c] * (sk // HEAD_DIM), axis=1)
                parts.append(jnp.exp2(z[:, cols] + r + cc))
                c = c + jnp.broadcast_to(r[:, 0:1], c.shape)
            a = jnp.concatenate(parts[::-1], axis=1)
            if mask is not None:
                a = jnp.where(mask, a, 0.0)
            acc_s[hh] += _dot(a.astype(BF16), vb)
            c_s[hh] = c

    row = lax.broadcasted_iota(jnp.int32, (tq, tq), 0)
    col = lax.broadcasted_iota(jnp.int32, (tq, tq), 1)
    tile(pl.multiple_of(qi * tq, tq), col < row)

    def body(j, carry):
        tile(pl.multiple_of((qi - 1 - j) * tq, tq), None)
        return carry

    lax.fori_loop(0, qi, body, 0)
    for hh in range(hpb):
        o_ref[:, hh * HEAD_DIM:(hh + 1) * HEAD_DIM] = acc_s[hh].astype(o_ref.dtype)


def sb_core(qkv, tq=512, sk=256, hpb=2):
    s, d3 = qkv.shape
    d = d3 // 3
    groups = d // (hpb * HEAD_DIM)
    j = jnp.arange(sk)
    w = (j[:, None] >= j[None, :]).astype(BF16)
    kern = functools.partial(_sb_kernel, tq=tq, sk=sk, hpb=hpb)
    wide = hpb * HEAD_DIM
    return pl.pallas_call(
        kern,
        out_shape=jax.ShapeDtypeStruct((s, d), BF16),
        grid=(groups, s // tq),
        in_specs=[pl.BlockSpec((tq, wide), lambda h, i: (i, h)),
                  pl.BlockSpec((s, wide), lambda h, i: (0, groups + h)),
                  pl.BlockSpec((s, wide), lambda h, i: (0, 2 * groups + h)),
                  pl.BlockSpec((sk, sk), lambda h, i: (0, 0))],
        out_specs=pl.BlockSpec((tq, wide), lambda h, i: (i, h)),
        scratch_shapes=[pltpu.VMEM((hpb, tq, HEAD_DIM), F32),
                        pltpu.VMEM((hpb, tq, HEAD_DIM), F32)],
        compiler_params=_params("parallel", "arbitrary"),
        name="sb_core",
    )(qkv, qkv, qkv, w)


def _ffn_half_step(h, norm_w, w_gu, w_down, layer):
    f = w_down.shape[1]
    fp = -(-f // FF_PAD) * FF_PAD
    gu = cast_col_groups(w_gu, layer, BF16, groups=2, tn=FF_TILE, cols_out=fp)
    down = cast_rows(w_down, layer, BF16, tr=FF_TILE, rows_out=fp)
    u = rmsnorm(h, norm_w[layer], BF16)
    a = ffn_up(u, gu)
    return matmul_residual(a, down, h, 0.5, tk=fp // 4)


def _cast_weight(w, layer):
    n = w.shape[2]
    tr = 8
    while 2 * tr * n <= 2 * 1024 * 1024 and w.shape[1] % (2 * tr) == 0:
        tr *= 2
    return cast_rows(w, layer, BF16, tr=tr)


def kernel(x, ffn1_norm, ffn1_w_gu, ffn1_w_down, mix_norm, ffn2_norm, ffn2_w_gu, ffn2_w_down,
           hgrn_w_in, hgrn_lb_logits, hgrn_gnorm, hgrn_w_out, sb_w_in, sb_w_out, final_norm):
    b, s, d = x.shape
    depth = ffn1_norm.shape[0]
    outs = []
    for bi in range(b):
        h = x[bi]
        for i in range(depth):
            h = _ffn_half_step(h, ffn1_norm, ffn1_w_gu, ffn1_w_down, i)
            u = rmsnorm(h, mix_norm[i], BF16)
            j = i // 2
            if i % 2 == 0:
                proj = matmul(u, _cast_weight(hgrn_w_in, j), F32)
                o = hgrn_core(proj, hgrn_lb_logits, hgrn_gnorm[j], layer=i)
                h = matmul_residual(o, _cast_weight(hgrn_w_out, j), h, 1.0)
            else:
                qkv = matmul(u, _cast_weight(sb_w_in, j), BF16, scaled_cols=d,
                             scale=math.log2(math.e) / math.sqrt(HEAD_DIM))
                o = sb_core(qkv)
                h = matmul_residual(o, _cast_weight(sb_w_out, j), h, 1.0)
            h = _ffn_half_step(h, ffn2_norm, ffn2_w_gu, ffn2_w_down, i)
        outs.append(rmsnorm(h, final_norm, x.dtype))
    return jnp.stack(outs, axis=0)
```

```python
import functools
import math

import jax
import jax.numpy as jnp
from jax import lax
from jax.experimental import pallas as pl
from jax.experimental.pallas import tpu as pltpu

EPS = 1e-6
CHUNK = 64
SUB = 16
HEAD_DIM = 128
FF_PAD = 1024
FF_TILE = 256

VMEM_LIMIT_BYTES = 56 * 1024 * 1024

BF16 = jnp.bfloat16
F32 = jnp.float32


def _params(*sem):
    return pltpu.CompilerParams(dimension_semantics=sem, vmem_limit_bytes=VMEM_LIMIT_BYTES)


def _dot(a, b):
    return jnp.dot(a, b, preferred_element_type=F32)


def _dot_nt(a, b):
    return lax.dot_general(a, b, (((1,), (1,)), ((), ())), preferred_element_type=F32)


def _dot_tn(a, b):
    return lax.dot_general(a, b, (((0,), (0,)), ((), ())), preferred_element_type=F32)


def _sigmoid(x):
    return 1.0 / (1.0 + jnp.exp(-x))


def _silu(x):
    return x * _sigmoid(x)


def _rmsnorm_kernel(x_ref, w_ref, o_ref):
    x = x_ref[...]
    y = x * lax.rsqrt(jnp.mean(x * x, axis=-1, keepdims=True) + EPS)
    o_ref[...] = (y * w_ref[...]).astype(o_ref.dtype)


def rmsnorm(x, w, out_dtype, tm=256):
    s, d = x.shape
    return pl.pallas_call(
        _rmsnorm_kernel,
        out_shape=jax.ShapeDtypeStruct((s, d), out_dtype),
        grid=(s // tm,),
        in_specs=[pl.BlockSpec((tm, d), lambda i: (i, 0)),
                  pl.BlockSpec((1, d), lambda i: (0, 0))],
        out_specs=pl.BlockSpec((tm, d), lambda i: (i, 0)),
        compiler_params=_params("parallel"),
        name="rmsnorm",
    )(x, w.reshape(1, d))


def _cast_kernel(x_ref, o_ref, *, axis, n_valid):
    x = x_ref[...].astype(o_ref.dtype)
    if n_valid is not None:
        x = jnp.where(pl.program_id(axis) < n_valid, x, jnp.zeros_like(x))
    o_ref[...] = x


def cast_rows(w, layer, dtype, tr, rows_out=None):
    _, r, n = w.shape
    rows_out = r if rows_out is None else rows_out
    n_valid = r // tr
    return pl.pallas_call(
        functools.partial(_cast_kernel, axis=0, n_valid=None if rows_out == r else n_valid),
        out_shape=jax.ShapeDtypeStruct((rows_out, n), dtype),
        grid=(rows_out // tr,),
        in_specs=[pl.BlockSpec((None, tr, n), lambda i: (layer, jnp.minimum(i, n_valid - 1), 0))],
        out_specs=pl.BlockSpec((tr, n), lambda i: (i, 0)),
        compiler_params=_params("parallel"),
        name="cast_rows",
    )(w)


def cast_col_groups(w, layer, dtype, groups, tn, cols_out):
    _, r, gf = w.shape
    f = gf // groups
    n_valid = f // tn
    n_out = cols_out // tn
    return pl.pallas_call(
        functools.partial(_cast_kernel, axis=1, n_valid=n_valid),
        out_shape=jax.ShapeDtypeStruct((r, groups * cols_out), dtype),
        grid=(groups, n_out),
        in_specs=[pl.BlockSpec((None, r, tn),
                               lambda g, j: (layer, 0, g * n_valid + jnp.minimum(j, n_valid - 1)))],
        out_specs=pl.BlockSpec((r, tn), lambda g, j: (0, g * n_out + j)),
        compiler_params=_params("parallel", "parallel"),
        name="cast_col_groups",
    )(w)


def _matmul_kernel(x_ref, w_ref, o_ref, *, scaled_blocks, scale):
    acc = _dot(x_ref[...], w_ref[...].astype(BF16))
    if scaled_blocks:
        acc = acc * jnp.where(pl.program_id(1) < scaled_blocks, scale, 1.0)
    o_ref[...] = acc.astype(o_ref.dtype)


def matmul(x, w, layer, out_dtype, tm=1024, tn=512, scaled_cols=0, scale=1.0):
    m, k = x.shape
    n = w.shape[2]
    return pl.pallas_call(
        functools.partial(_matmul_kernel, scaled_blocks=scaled_cols // tn, scale=scale),
        out_shape=jax.ShapeDtypeStruct((m, n), out_dtype),
        grid=(m // tm, n // tn),
        in_specs=[pl.BlockSpec((tm, k), lambda i, j: (i, 0)),
                  pl.BlockSpec((None, k, tn), lambda i, j: (layer, 0, j))],
        out_specs=pl.BlockSpec((tm, tn), lambda i, j: (i, j)),
        compiler_params=_params("parallel", "parallel"),
        name="matmul",
    )(x, w)


def _matmul_residual_kernel(x_ref, w_ref, h_ref, o_ref, *, scale, nk):
    kk = pl.program_id(2)
    part = _dot(x_ref[...], w_ref[...])

    @pl.when(kk == 0)
    def _():
        o_ref[...] = part

    @pl.when(kk > 0)
    def _():
        o_ref[...] += part

    @pl.when(kk == nk - 1)
    def _():
        o_ref[...] = h_ref[...] + scale * o_ref[...]


def matmul_residual(x, w, h, scale, tm=1024, tn=1024, tk=None):
    m, k = x.shape
    n = w.shape[1]
    tk = k if tk is None else tk
    nk = k // tk
    return pl.pallas_call(
        functools.partial(_matmul_residual_kernel, scale=scale, nk=nk),
        out_shape=jax.ShapeDtypeStruct((m, n), F32),
        grid=(m // tm, n // tn, nk),
        in_specs=[pl.BlockSpec((tm, tk), lambda i, j, kk: (i, kk)),
                  pl.BlockSpec((tk, tn), lambda i, j, kk: (kk, j)),
                  pl.BlockSpec((tm, tn), lambda i, j, kk: (i, j))],
        out_specs=pl.BlockSpec((tm, tn), lambda i, j, kk: (i, j)),
        compiler_params=_params("parallel", "parallel", "arbitrary"),
        name="matmul_residual",
    )(x, w, h)


def _ffn_up_kernel(x_ref, wg_ref, wu_ref, o_ref, *, n_valid):
    j = pl.program_id(1)

    @pl.when(j < n_valid)
    def _():
        x = x_ref[...]
        g = _dot(x, wg_ref[...].astype(BF16))
        u = _dot(x, wu_ref[...].astype(BF16))
        o_ref[...] = (_silu(g) * u).astype(o_ref.dtype)

    @pl.when(j >= n_valid)
    def _():
        o_ref[...] = jnp.zeros_like(o_ref)


def ffn_up(x, w_gu, layer, f_out, tm=2048, tn=256):
    m, k = x.shape
    f = w_gu.shape[2] // 2
    n_valid = f // tn
    return pl.pallas_call(
        functools.partial(_ffn_up_kernel, n_valid=n_valid),
        out_shape=jax.ShapeDtypeStruct((m, f_out), BF16),
        grid=(m // tm, f_out // tn),
        in_specs=[pl.BlockSpec((tm, k), lambda i, j: (i, 0), pipeline_mode=pl.Buffered(1)),
                  pl.BlockSpec((None, k, tn), lambda i, j: (layer, 0, jnp.minimum(j, n_valid - 1))),
                  pl.BlockSpec((None, k, tn),
                               lambda i, j: (layer, 0, n_valid + jnp.minimum(j, n_valid - 1)))],
        out_specs=pl.BlockSpec((tm, tn), lambda i, j: (i, j)),
        compiler_params=_params("parallel", "arbitrary"),
        name="ffn_up",
    )(x, w_gu, w_gu)


def _split3(x):
    a = x.astype(BF16)
    r = x - a.astype(F32)
    b = r.astype(BF16)
    c = (r - b.astype(F32)).astype(BF16)
    return a, b, c


def _hgrn_kernel(q_ref, f_ref, i_ref, gate_ref, lbl_ref, gn_ref, u_ref, o_ref,
                 q_s, k_s, b_s, p_s, st_s, d_s, *, layer, tc):
    n_chunks = tc // CHUNK
    n_sub = CHUNK // SUB

    @pl.when(pl.program_id(1) == 0)
    def _():
        st_s[...] = jnp.zeros_like(st_s)

    lbl = lbl_ref[...]
    e = jnp.exp(lbl - jnp.max(lbl, axis=0, keepdims=True))
    lb = jnp.sum(e[:layer + 1], axis=0, keepdims=True) / jnp.sum(e, axis=0, keepdims=True)

    row = lax.broadcasted_iota(jnp.int32, (CHUNK, CHUNK), 0)
    col = lax.broadcasted_iota(jnp.int32, (CHUNK, CHUNK), 1)
    tril = jnp.where(col <= row, 1.0, 0.0).astype(BF16)
    tril3 = jnp.concatenate([tril, tril, tril], axis=1)
    diag_mask = ((row // SUB) == (col // SUB)) & (col <= row)
    off_mask = (col // SUB) < (row // SUB)


    for c in range(n_chunks):
        rows = slice(c * CHUNK, (c + 1) * CHUNK)
        q = _silu(q_ref[rows, :])
        f = lb + (1.0 - lb) * _sigmoid(f_ref[rows, :])
        k = 1.0 - f
        g1, g2, g3 = _split3(jnp.log2(f))
        b = _dot(tril3, jnp.concatenate([g1, g2, g3], axis=0))
        q_s[rows, :] = q
        k_s[rows, :] = k
        b_s[rows, :] = b
        for i in range(n_sub):
            r0 = c * CHUNK + i * SUB
            qi = q[i * SUB:(i + 1) * SUB]
            bi = b[i * SUB:(i + 1) * SUB]
            for s in range(SUB):
                r = i * SUB + s
                p = qi * k[r:r + 1] * jnp.exp2(jnp.minimum(bi - b[r:r + 1], 0.0))
                p_s[r0:r0 + SUB, s * HEAD_DIM:(s + 1) * HEAD_DIM] = p.astype(BF16)

    d_s[...] = _dot(p_s[...], u_ref[...])

    gn = gn_ref[...]
    st = st_s[...]
    for c in range(n_chunks):
        rows = slice(c * CHUNK, (c + 1) * CHUNK)
        q = q_s[rows, :]
        k = k_s[rows, :]
        b = b_s[rows, :]
        vb = i_ref[rows, :].astype(BF16)

        o = _dot_nt((q * jnp.exp2(b)).astype(BF16), st.astype(BF16))

        blocks = [jnp.zeros((SUB, CHUNK), F32)]
        for i in range(1, n_sub):
            bref = b[i * SUB - 1:i * SUB]
            qs = q[i * SUB:(i + 1) * SUB] * jnp.exp2(b[i * SUB:(i + 1) * SUB] - bref)
            ks = k * jnp.exp2(jnp.minimum(bref - b, 0.0))
            blocks.append(_dot_nt(qs.astype(BF16), ks.astype(BF16)))
        a_off = jnp.concatenate(blocks, axis=0)
        d = d_s[rows, :][:, :CHUNK]
        a = jnp.where(diag_mask, d, jnp.where(off_mask, a_off, 0.0))
        o = o + _dot(a.astype(BF16), vb)

        b_last = b[CHUNK - 1:CHUNK]
        k_dec = k * jnp.exp2(b_last - b)
        st = st * jnp.exp2(b_last) + _dot_tn(vb, k_dec.astype(BF16))

        y = o * lax.rsqrt(jnp.mean(o * o, axis=-1, keepdims=True) + EPS) * gn
        o_ref[rows, :] = (y * _silu(gate_ref[rows, :])).astype(o_ref.dtype)
    st_s[...] = st


def hgrn_core(proj, lb_logits, gnorm, layer, tc=512):
    s, d4 = proj.shape
    d = d4 // 4
    heads = d // HEAD_DIM
    n_l = lb_logits.shape[0]
    r = jnp.arange(SUB * HEAD_DIM)[:, None] // HEAD_DIM
    c = jnp.arange(HEAD_DIM)[None, :] % SUB
    u = (r == c).astype(BF16)

    def col(off):
        return pl.BlockSpec((tc, HEAD_DIM), lambda h, t: (t, off * heads + h))

    return pl.pallas_call(
        functools.partial(_hgrn_kernel, layer=layer, tc=tc),
        out_shape=jax.ShapeDtypeStruct((s, d), BF16),
        grid=(heads, s // tc),
        in_specs=[col(0), col(1), col(2), col(3),
                  pl.BlockSpec((n_l, HEAD_DIM), lambda h, t: (0, h)),
                  pl.BlockSpec((1, HEAD_DIM), lambda h, t: (0, 0)),
                  pl.BlockSpec((SUB * HEAD_DIM, HEAD_DIM), lambda h, t: (0, 0))],
        out_specs=pl.BlockSpec((tc, HEAD_DIM), lambda h, t: (t, h)),
        scratch_shapes=[pltpu.VMEM((tc, HEAD_DIM), F32),
                        pltpu.VMEM((tc, HEAD_DIM), F32),
                        pltpu.VMEM((tc, HEAD_DIM), F32),
                        pltpu.VMEM((tc, SUB * HEAD_DIM), BF16),
                        pltpu.VMEM((HEAD_DIM, HEAD_DIM), F32),
                        pltpu.VMEM((tc, HEAD_DIM), F32)],
        compiler_params=_params("parallel", "arbitrary"),
        name="hgrn_core",
    )(proj, proj, proj, proj, lb_logits, gnorm.reshape(1, HEAD_DIM), u)


def _sb_kernel(q_ref, k_ref, v_ref, w_ref, o_ref, acc_s, c_s, *, tq, sk, hpb):
    qi = pl.program_id(1)
    w = w_ref[...]
    acc_s[...] = jnp.zeros_like(acc_s)
    c_s[...] = jnp.zeros_like(c_s)
    n_sub = tq // sk

    def tile(k0, mask):
        for hh in range(hpb):
            hc = slice(hh * HEAD_DIM, (hh + 1) * HEAD_DIM)
            kb = k_ref[pl.ds(k0, tq), hc]
            vb = v_ref[pl.ds(k0, tq), hc]
            z = _dot_nt(q_ref[:, hc], kb)
            nz = -z
            lk = jnp.minimum(nz, 0.0) - jnp.log2(1.0 + jnp.exp2(jnp.minimum(z, nz)))
            if mask is not None:
                lk = jnp.where(mask, lk, 0.0)
            lkb = lk.astype(BF16)
            c = c_s[hh]
            parts = []
            for s in range(n_sub - 1, -1, -1):
                cols = slice(s * sk, (s + 1) * sk)
                r = _dot(lkb[:, cols], w)
                cc = jnp.concatenate([c] * (sk // HEAD_DIM), axis=1)
                parts.append(jnp.exp2(z[:, cols] + r + cc))
                c = c + jnp.broadcast_to(r[:, 0:1], c.shape)
            a = jnp.concatenate(parts[::-1], axis=1)
            if mask is not None:
                a = jnp.where(mask, a, 0.0)
            acc_s[hh] += _dot(a.astype(BF16), vb)
            c_s[hh] = c

    row = lax.broadcasted_iota(jnp.int32, (tq, tq), 0)
    col = lax.broadcasted_iota(jnp.int32, (tq, tq), 1)
    tile(pl.multiple_of(qi * tq, tq), col < row)

    def body(j, carry):
        tile(pl.multiple_of((qi - 1 - j) * tq, tq), None)
        return carry

    lax.fori_loop(0, qi, body, 0)
    for hh in range(hpb):
        o_ref[:, hh * HEAD_DIM:(hh + 1) * HEAD_DIM] = acc_s[hh].astype(o_ref.dtype)


def sb_core(qkv, tq=512, sk=256, hpb=2):
    s, d3 = qkv.shape
    d = d3 // 3
    groups = d // (hpb * HEAD_DIM)
    j = jnp.arange(sk)
    w = (j[:, None] >= j[None, :]).astype(BF16)
    kern = functools.partial(_sb_kernel, tq=tq, sk=sk, hpb=hpb)
    wide = hpb * HEAD_DIM
    return pl.pallas_call(
        kern,
        out_shape=jax.ShapeDtypeStruct((s, d), BF16),
        grid=(groups, s // tq),
        in_specs=[pl.BlockSpec((tq, wide), lambda h, i: (i, h)),
                  pl.BlockSpec((s, wide), lambda h, i: (0, groups + h)),
                  pl.BlockSpec((s, wide), lambda h, i: (0, 2 * groups + h)),
                  pl.BlockSpec((sk, sk), lambda h, i: (0, 0))],
        out_specs=pl.BlockSpec((tq, wide), lambda h, i: (i, h)),
        scratch_shapes=[pltpu.VMEM((hpb, tq, HEAD_DIM), F32),
                        pltpu.VMEM((hpb, tq, HEAD_DIM), F32)],
        compiler_params=_params("parallel", "arbitrary"),
        name="sb_core",
    )(qkv, qkv, qkv, w)


def _ffn_half_step(h, norm_w, w_gu, w_down, layer):
    f = w_down.shape[1]
    fp = -(-f // FF_PAD) * FF_PAD
    down = cast_rows(w_down, layer, BF16, tr=FF_TILE, rows_out=fp)
    u = rmsnorm(h, norm_w[layer], BF16)
    a = ffn_up(u, w_gu, layer, fp, tn=FF_TILE)
    return matmul_residual(a, down, h, 0.5, tk=fp // 4)


def _cast_weight(w, layer):
    n = w.shape[2]
    tr = 8
    while 2 * tr * n <= 2 * 1024 * 1024 and w.shape[1] % (2 * tr) == 0:
        tr *= 2
    return cast_rows(w, layer, BF16, tr=tr)


def kernel(x, ffn1_norm, ffn1_w_gu, ffn1_w_down, mix_norm, ffn2_norm, ffn2_w_gu, ffn2_w_down,
           hgrn_w_in, hgrn_lb_logits, hgrn_gnorm, hgrn_w_out, sb_w_in, sb_w_out, final_norm):
    b, s, d = x.shape
    depth = ffn1_norm.shape[0]
    outs = []
    for bi in range(b):
        h = x[bi]
        for i in range(depth):
            h = _ffn_half_step(h, ffn1_norm, ffn1_w_gu, ffn1_w_down, i)
            u = rmsnorm(h, mix_norm[i], BF16)
            j = i // 2
            if i % 2 == 0:
                proj = matmul(u, hgrn_w_in, j, F32)
                o = hgrn_core(proj, hgrn_lb_logits, hgrn_gnorm[j], layer=i)
                h = matmul_residual(o, _cast_weight(hgrn_w_out, j), h, 1.0)
            else:
                qkv = matmul(u, sb_w_in, j, BF16, scaled_cols=d,
                             scale=math.log2(math.e) / math.sqrt(HEAD_DIM))
                o = sb_core(qkv)
                h = matmul_residual(o, _cast_weight(sb_w_out, j), h, 1.0)
            h = _ffn_half_step(h, ffn2_norm, ffn2_w_gu, ffn2_w_down, i)
        outs.append(rmsnorm(h, final_norm, x.dtype))
    return jnp.stack(outs, axis=0)
```

```python
import functools
import math

import jax
import jax.numpy as jnp
from jax import lax
from jax.experimental import pallas as pl
from jax.experimental.pallas import tpu as pltpu

EPS = 1e-6
CHUNK = 64
SUB = 16
HEAD_DIM = 128
FF_PAD = 1024
FF_TILE = 256
F32_MIN_SUBNORMAL_LOG2 = -149.0
SB_UNDERFLOW_LOG2 = F32_MIN_SUBNORMAL_LOG2 - 43.0

VMEM_LIMIT_BYTES = 56 * 1024 * 1024

BF16 = jnp.bfloat16
F32 = jnp.float32


def _params(*sem):
    return pltpu.CompilerParams(dimension_semantics=sem, vmem_limit_bytes=VMEM_LIMIT_BYTES)


def _dot(a, b):
    return jnp.dot(a, b, preferred_element_type=F32)


def _dot_nt(a, b):
    return lax.dot_general(a, b, (((1,), (1,)), ((), ())), preferred_element_type=F32)


def _dot_tn(a, b):
    return lax.dot_general(a, b, (((0,), (0,)), ((), ())), preferred_element_type=F32)


def _sigmoid(x):
    return 1.0 / (1.0 + jnp.exp(-x))


def _silu(x):
    return x * _sigmoid(x)


def _rmsnorm_kernel(x_ref, w_ref, o_ref):
    x = x_ref[...]
    y = x * lax.rsqrt(jnp.mean(x * x, axis=-1, keepdims=True) + EPS)
    o_ref[...] = (y * w_ref[...]).astype(o_ref.dtype)


def rmsnorm(x, w, out_dtype, tm=256):
    s, d = x.shape
    return pl.pallas_call(
        _rmsnorm_kernel,
        out_shape=jax.ShapeDtypeStruct((s, d), out_dtype),
        grid=(s // tm,),
        in_specs=[pl.BlockSpec((tm, d), lambda i: (i, 0)),
                  pl.BlockSpec((1, d), lambda i: (0, 0))],
        out_specs=pl.BlockSpec((tm, d), lambda i: (i, 0)),
        compiler_params=_params("parallel"),
        name="rmsnorm",
    )(x, w.reshape(1, d))


def _cast_kernel(x_ref, o_ref, *, axis, n_valid):
    x = x_ref[...].astype(o_ref.dtype)
    if n_valid is not None:
        x = jnp.where(pl.program_id(axis) < n_valid, x, jnp.zeros_like(x))
    o_ref[...] = x


def cast_rows(w, layer, dtype, tr, rows_out=None):
    _, r, n = w.shape
    rows_out = r if rows_out is None else rows_out
    n_valid = r // tr
    return pl.pallas_call(
        functools.partial(_cast_kernel, axis=0, n_valid=None if rows_out == r else n_valid),
        out_shape=jax.ShapeDtypeStruct((rows_out, n), dtype),
        grid=(rows_out // tr,),
        in_specs=[pl.BlockSpec((None, tr, n), lambda i: (layer, jnp.minimum(i, n_valid - 1), 0))],
        out_specs=pl.BlockSpec((tr, n), lambda i: (i, 0)),
        compiler_params=_params("parallel"),
        name="cast_rows",
    )(w)


def cast_col_groups(w, layer, dtype, groups, tn, cols_out):
    _, r, gf = w.shape
    f = gf // groups
    n_valid = f // tn
    n_out = cols_out // tn
    return pl.pallas_call(
        functools.partial(_cast_kernel, axis=1, n_valid=n_valid),
        out_shape=jax.ShapeDtypeStruct((r, groups * cols_out), dtype),
        grid=(groups, n_out),
        in_specs=[pl.BlockSpec((None, r, tn),
                               lambda g, j: (layer, 0, g * n_valid + jnp.minimum(j, n_valid - 1)))],
        out_specs=pl.BlockSpec((r, tn), lambda g, j: (0, g * n_out + j)),
        compiler_params=_params("parallel", "parallel"),
        name="cast_col_groups",
    )(w)


def _matmul_kernel(x_ref, w_ref, o_ref, *, scaled_blocks, scale):
    acc = _dot(x_ref[...], w_ref[...].astype(BF16))
    if scaled_blocks:
        acc = acc * jnp.where(pl.program_id(1) < scaled_blocks, scale, 1.0)
    o_ref[...] = acc.astype(o_ref.dtype)


def matmul(x, w, layer, out_dtype, tm=2048, tn=512, scaled_cols=0, scale=1.0):
    m, k = x.shape
    n = w.shape[2]
    return pl.pallas_call(
        functools.partial(_matmul_kernel, scaled_blocks=scaled_cols // tn, scale=scale),
        out_shape=jax.ShapeDtypeStruct((m, n), out_dtype),
        grid=(m // tm, n // tn),
        in_specs=[pl.BlockSpec((tm, k), lambda i, j: (i, 0), pipeline_mode=pl.Buffered(1)),
                  pl.BlockSpec((None, k, tn), lambda i, j: (layer, 0, j))],
        out_specs=pl.BlockSpec((tm, tn), lambda i, j: (i, j)),
        compiler_params=_params("parallel", "arbitrary"),
        name="matmul",
    )(x, w)


def _matmul_residual_kernel(x_ref, w_ref, h_ref, o_ref, *, scale, nk):
    kk = pl.program_id(2)
    part = _dot(x_ref[...], w_ref[...])

    @pl.when(kk == 0)
    def _():
        o_ref[...] = part

    @pl.when(kk > 0)
    def _():
        o_ref[...] += part

    @pl.when(kk == nk - 1)
    def _():
        o_ref[...] = h_ref[...] + scale * o_ref[...]


def matmul_residual(x, w, h, scale, tm=1024, tn=1024, tk=None):
    m, k = x.shape
    n = w.shape[1]
    tk = k if tk is None else tk
    nk = k // tk
    return pl.pallas_call(
        functools.partial(_matmul_residual_kernel, scale=scale, nk=nk),
        out_shape=jax.ShapeDtypeStruct((m, n), F32),
        grid=(m // tm, n // tn, nk),
        in_specs=[pl.BlockSpec((tm, tk), lambda i, j, kk: (i, kk)),
                  pl.BlockSpec((tk, tn), lambda i, j, kk: (kk, j)),
                  pl.BlockSpec((tm, tn), lambda i, j, kk: (i, j))],
        out_specs=pl.BlockSpec((tm, tn), lambda i, j, kk: (i, j)),
        compiler_params=_params("parallel", "parallel", "arbitrary"),
        name="matmul_residual",
    )(x, w, h)


def _ffn_up_kernel(x_ref, wg_ref, wu_ref, o_ref, *, n_valid):
    j = pl.program_id(1)

    @pl.when(j < n_valid)
    def _():
        x = x_ref[...]
        g = _dot(x, wg_ref[...].astype(BF16))
        u = _dot(x, wu_ref[...].astype(BF16))
        o_ref[...] = (_silu(g) * u).astype(o_ref.dtype)

    @pl.when(j >= n_valid)
    def _():
        o_ref[...] = jnp.zeros_like(o_ref)


def ffn_up(x, w_gu, layer, f_out, tm=2048, tn=256):
    m, k = x.shape
    f = w_gu.shape[2] // 2
    n_valid = f // tn
    return pl.pallas_call(
        functools.partial(_ffn_up_kernel, n_valid=n_valid),
        out_shape=jax.ShapeDtypeStruct((m, f_out), BF16),
        grid=(m // tm, f_out // tn),
        in_specs=[pl.BlockSpec((tm, k), lambda i, j: (i, 0), pipeline_mode=pl.Buffered(1)),
                  pl.BlockSpec((None, k, tn), lambda i, j: (layer, 0, jnp.minimum(j, n_valid - 1))),
                  pl.BlockSpec((None, k, tn),
                               lambda i, j: (layer, 0, n_valid + jnp.minimum(j, n_valid - 1)))],
        out_specs=pl.BlockSpec((tm, tn), lambda i, j: (i, j)),
        compiler_params=_params("parallel", "arbitrary"),
        name="ffn_up",
    )(x, w_gu, w_gu)


def _split3(x):
    a = x.astype(BF16)
    r = x - a.astype(F32)
    b = r.astype(BF16)
    c = (r - b.astype(F32)).astype(BF16)
    return a, b, c


def _hgrn_kernel(q_ref, f_ref, i_ref, gate_ref, lbl_ref, gn_ref, u_ref, o_ref,
                 q_s, k_s, b_s, p_s, st_s, d_s, *, layer, tc):
    n_chunks = tc // CHUNK
    n_sub = CHUNK // SUB

    @pl.when(pl.program_id(1) == 0)
    def _():
        st_s[...] = jnp.zeros_like(st_s)

    lbl = lbl_ref[...]
    e = jnp.exp(lbl - jnp.max(lbl, axis=0, keepdims=True))
    lb = jnp.sum(e[:layer + 1], axis=0, keepdims=True) / jnp.sum(e, axis=0, keepdims=True)

    row = lax.broadcasted_iota(jnp.int32, (CHUNK, CHUNK), 0)
    col = lax.broadcasted_iota(jnp.int32, (CHUNK, CHUNK), 1)
    tril = jnp.where(col <= row, 1.0, 0.0).astype(BF16)
    tril3 = jnp.concatenate([tril, tril, tril], axis=1)
    diag_mask = ((row // SUB) == (col // SUB)) & (col <= row)
    off_mask = (col // SUB) < (row // SUB)


    for c in range(n_chunks):
        rows = slice(c * CHUNK, (c + 1) * CHUNK)
        q = _silu(q_ref[rows, :])
        f = lb + (1.0 - lb) * _sigmoid(f_ref[rows, :])
        k = 1.0 - f
        g1, g2, g3 = _split3(jnp.log2(f))
        b = _dot(tril3, jnp.concatenate([g1, g2, g3], axis=0))
        q_s[rows, :] = q
        k_s[rows, :] = k
        b_s[rows, :] = b
        for i in range(n_sub):
            r0 = c * CHUNK + i * SUB
            qi = q[i * SUB:(i + 1) * SUB]
            bi = b[i * SUB:(i + 1) * SUB]
            for s in range(SUB):
                r = i * SUB + s
                p = qi * k[r:r + 1] * jnp.exp2(jnp.minimum(bi - b[r:r + 1], 0.0))
                p_s[r0:r0 + SUB, s * HEAD_DIM:(s + 1) * HEAD_DIM] = p.astype(BF16)

    d_s[...] = _dot(p_s[...], u_ref[...])

    gn = gn_ref[...]
    st = st_s[...]
    for c in range(n_chunks):
        rows = slice(c * CHUNK, (c + 1) * CHUNK)
        q = q_s[rows, :]
        k = k_s[rows, :]
        b = b_s[rows, :]
        vb = i_ref[rows, :].astype(BF16)

        o = _dot_nt((q * jnp.exp2(b)).astype(BF16), st.astype(BF16))

        blocks = [jnp.zeros((SUB, CHUNK), F32)]
        for i in range(1, n_sub):
            bref = b[i * SUB - 1:i * SUB]
            qs = q[i * SUB:(i + 1) * SUB] * jnp.exp2(b[i * SUB:(i + 1) * SUB] - bref)
            ks = k * jnp.exp2(jnp.minimum(bref - b, 0.0))
            blocks.append(_dot_nt(qs.astype(BF16), ks.astype(BF16)))
        a_off = jnp.concatenate(blocks, axis=0)
        d = d_s[rows, :][:, :CHUNK]
        a = jnp.where(diag_mask, d, jnp.where(off_mask, a_off, 0.0))
        o = o + _dot(a.astype(BF16), vb)

        b_last = b[CHUNK - 1:CHUNK]
        k_dec = k * jnp.exp2(b_last - b)
        st = st * jnp.exp2(b_last) + _dot_tn(vb, k_dec.astype(BF16))

        y = o * lax.rsqrt(jnp.mean(o * o, axis=-1, keepdims=True) + EPS) * gn
        o_ref[rows, :] = (y * _silu(gate_ref[rows, :])).astype(o_ref.dtype)
    st_s[...] = st


def hgrn_core(proj, lb_logits, gnorm, layer, tc=512):
    s, d4 = proj.shape
    d = d4 // 4
    heads = d // HEAD_DIM
    n_l = lb_logits.shape[0]
    r = jnp.arange(SUB * HEAD_DIM)[:, None] // HEAD_DIM
    c = jnp.arange(HEAD_DIM)[None, :] % SUB
    u = (r == c).astype(BF16)

    def col(off):
        return pl.BlockSpec((tc, HEAD_DIM), lambda h, t: (t, off * heads + h))

    return pl.pallas_call(
        functools.partial(_hgrn_kernel, layer=layer, tc=tc),
        out_shape=jax.ShapeDtypeStruct((s, d), BF16),
        grid=(heads, s // tc),
        in_specs=[col(0), col(1), col(2), col(3),
                  pl.BlockSpec((n_l, HEAD_DIM), lambda h, t: (0, h)),
                  pl.BlockSpec((1, HEAD_DIM), lambda h, t: (0, 0)),
                  pl.BlockSpec((SUB * HEAD_DIM, HEAD_DIM), lambda h, t: (0, 0))],
        out_specs=pl.BlockSpec((tc, HEAD_DIM), lambda h, t: (t, h)),
        scratch_shapes=[pltpu.VMEM((tc, HEAD_DIM), F32),
                        pltpu.VMEM((tc, HEAD_DIM), F32),
                        pltpu.VMEM((tc, HEAD_DIM), F32),
                        pltpu.VMEM((tc, SUB * HEAD_DIM), BF16),
                        pltpu.VMEM((HEAD_DIM, HEAD_DIM), F32),
                        pltpu.VMEM((tc, HEAD_DIM), F32)],
        compiler_params=_params("parallel", "arbitrary"),
        name="hgrn_core",
    )(proj, proj, proj, proj, lb_logits, gnorm.reshape(1, HEAD_DIM), u)


def _sb_kernel(q_ref, k_ref, v_ref, w_ref, o_ref, acc_s, c_s, *, tq, sk, hpb):
    qi = pl.program_id(1)
    w = w_ref[...]
    acc_s[...] = jnp.zeros_like(acc_s)
    c_s[...] = jnp.zeros_like(c_s)
    n_sub = tq // sk

    def tile(k0, mask):
        for hh in range(hpb):
            hc = slice(hh * HEAD_DIM, (hh + 1) * HEAD_DIM)
            kb = k_ref[pl.ds(k0, tq), hc]
            vb = v_ref[pl.ds(k0, tq), hc]
            z = _dot_nt(q_ref[:, hc], kb)
            nz = -z
            lk = jnp.minimum(nz, 0.0) - jnp.log2(1.0 + jnp.exp2(jnp.minimum(z, nz)))
            if mask is not None:
                lk = jnp.where(mask, lk, 0.0)
            lkb = lk.astype(BF16)
            c = c_s[hh]
            parts = []
            for s in range(n_sub - 1, -1, -1):
                cols = slice(s * sk, (s + 1) * sk)
                r = _dot(lkb[:, cols], w)
                cc = jnp.concatenate([c] * (sk // HEAD_DIM), axis=1)
                parts.append(jnp.exp2(z[:, cols] + r + cc))
                c = c + jnp.broadcast_to(r[:, 0:1], c.shape)
            a = jnp.concatenate(parts[::-1], axis=1)
            if mask is not None:
                a = jnp.where(mask, a, 0.0)
            acc_s[hh] += _dot(a.astype(BF16), vb)
            c_s[hh] = c

    row = lax.broadcasted_iota(jnp.int32, (tq, tq), 0)
    col = lax.broadcasted_iota(jnp.int32, (tq, tq), 1)
    tile(pl.multiple_of(qi * tq, tq), col < row)

    def cond(carry):
        j, c_max = carry
        return jnp.logical_and(j < qi, c_max > SB_UNDERFLOW_LOG2)

    def body(carry):
        j, _ = carry
        tile(pl.multiple_of((qi - 1 - j) * tq, tq), None)
        return j + 1, jnp.max(c_s[...])

    lax.while_loop(cond, body, (jnp.int32(0), jnp.max(c_s[...])))
    for hh in range(hpb):
        o_ref[:, hh * HEAD_DIM:(hh + 1) * HEAD_DIM] = acc_s[hh].astype(o_ref.dtype)


def sb_core(qkv, tq=512, sk=256, hpb=2):
    s, d3 = qkv.shape
    d = d3 // 3
    groups = d // (hpb * HEAD_DIM)
    j = jnp.arange(sk)
    w = (j[:, None] >= j[None, :]).astype(BF16)
    kern = functools.partial(_sb_kernel, tq=tq, sk=sk, hpb=hpb)
    wide = hpb * HEAD_DIM
    return pl.pallas_call(
        kern,
        out_shape=jax.ShapeDtypeStruct((s, d), BF16),
        grid=(groups, s // tq),
        in_specs=[pl.BlockSpec((tq, wide), lambda h, i: (i, h)),
                  pl.BlockSpec((s, wide), lambda h, i: (0, groups + h)),
                  pl.BlockSpec((s, wide), lambda h, i: (0, 2 * groups + h)),
                  pl.BlockSpec((sk, sk), lambda h, i: (0, 0))],
        out_specs=pl.BlockSpec((tq, wide), lambda h, i: (i, h)),
        scratch_shapes=[pltpu.VMEM((hpb, tq, HEAD_DIM), F32),
                        pltpu.VMEM((hpb, tq, HEAD_DIM), F32)],
        compiler_params=_params("parallel", "arbitrary"),
        name="sb_core",
    )(qkv, qkv, qkv, w)


def _ffn_half_step(h, norm_w, w_gu, w_down, layer):
    f = w_down.shape[1]
    fp = -(-f // FF_PAD) * FF_PAD
    down = cast_rows(w_down, layer, BF16, tr=FF_TILE, rows_out=fp)
    u = rmsnorm(h, norm_w[layer], BF16)
    a = ffn_up(u, w_gu, layer, fp, tn=FF_TILE)
    return matmul_residual(a, down, h, 0.5, tk=fp // 4)


def _cast_weight(w, layer):
    n = w.shape[2]
    tr = 8
    while 2 * tr * n <= 2 * 1024 * 1024 and w.shape[1] % (2 * tr) == 0:
        tr *= 2
    return cast_rows(w, layer, BF16, tr=tr)


def kernel(x, ffn1_norm, ffn1_w_gu, ffn1_w_down, mix_norm, ffn2_norm, ffn2_w_gu, ffn2_w_down,
           hgrn_w_in, hgrn_lb_logits, hgrn_gnorm, hgrn_w_out, sb_w_in, sb_w_out, final_norm):
    b, s, d = x.shape
    depth = ffn1_norm.shape[0]
    outs = []
    for bi in range(b):
        h = x[bi]
        for i in range(depth):
            h = _ffn_half_step(h, ffn1_norm, ffn1_w_gu, ffn1_w_down, i)
            u = rmsnorm(h, mix_norm[i], BF16)
            j = i // 2
            if i % 2 == 0:
                proj = matmul(u, hgrn_w_in, j, F32)
                o = hgrn_core(proj, hgrn_lb_logits, hgrn_gnorm[j], layer=i)
                h = matmul_residual(o, _cast_weight(hgrn_w_out, j), h, 1.0)
            else:
                qkv = matmul(u, sb_w_in, j, BF16, scaled_cols=d,
                             scale=math.log2(math.e) / math.sqrt(HEAD_DIM))
                o = sb_core(qkv)
                h = matmul_residual(o, _cast_weight(sb_w_out, j), h, 1.0)
            h = _ffn_half_step(h, ffn2_norm, ffn2_w_gu, ffn2_w_down, i)
        outs.append(rmsnorm(h, final_norm, x.dtype))
    return jnp.stack(outs, axis=0)
```

```python
import functools
import math

import jax
import jax.numpy as jnp
from jax import lax
from jax.experimental import pallas as pl
from jax.experimental.pallas import tpu as pltpu

EPS = 1e-6
CHUNK = 64
SUB = 8
HEAD_DIM = 128
FF_PAD = 1024
FF_TILE = 256
F32_MIN_SUBNORMAL_LOG2 = -149.0
SB_UNDERFLOW_LOG2 = F32_MIN_SUBNORMAL_LOG2 - 43.0

VMEM_LIMIT_BYTES = 56 * 1024 * 1024

BF16 = jnp.bfloat16
F32 = jnp.float32


def _params(*sem):
    return pltpu.CompilerParams(dimension_semantics=sem, vmem_limit_bytes=VMEM_LIMIT_BYTES)


def _dot(a, b):
    return jnp.dot(a, b, preferred_element_type=F32)


def _dot_nt(a, b):
    return lax.dot_general(a, b, (((1,), (1,)), ((), ())), preferred_element_type=F32)


def _dot_tn(a, b):
    return lax.dot_general(a, b, (((0,), (0,)), ((), ())), preferred_element_type=F32)


def _sigmoid(x):
    return 1.0 / (1.0 + jnp.exp(-x))


def _silu(x):
    return x * _sigmoid(x)


def _rmsnorm_kernel(x_ref, w_ref, o_ref):
    x = x_ref[...]
    y = x * lax.rsqrt(jnp.mean(x * x, axis=-1, keepdims=True) + EPS)
    o_ref[...] = (y * w_ref[...]).astype(o_ref.dtype)


def rmsnorm(x, w, out_dtype, tm=256):
    s, d = x.shape
    return pl.pallas_call(
        _rmsnorm_kernel,
        out_shape=jax.ShapeDtypeStruct((s, d), out_dtype),
        grid=(s // tm,),
        in_specs=[pl.BlockSpec((tm, d), lambda i: (i, 0)),
                  pl.BlockSpec((1, d), lambda i: (0, 0))],
        out_specs=pl.BlockSpec((tm, d), lambda i: (i, 0)),
        compiler_params=_params("parallel"),
        name="rmsnorm",
    )(x, w.reshape(1, d))


def _cast_kernel(x_ref, o_ref, *, axis, n_valid):
    x = x_ref[...].astype(o_ref.dtype)
    if n_valid is not None:
        x = jnp.where(pl.program_id(axis) < n_valid, x, jnp.zeros_like(x))
    o_ref[...] = x


def cast_rows(w, layer, dtype, tr, rows_out=None):
    _, r, n = w.shape
    rows_out = r if rows_out is None else rows_out
    n_valid = r // tr
    return pl.pallas_call(
        functools.partial(_cast_kernel, axis=0, n_valid=None if rows_out == r else n_valid),
        out_shape=jax.ShapeDtypeStruct((rows_out, n), dtype),
        grid=(rows_out // tr,),
        in_specs=[pl.BlockSpec((None, tr, n), lambda i: (layer, jnp.minimum(i, n_valid - 1), 0))],
        out_specs=pl.BlockSpec((tr, n), lambda i: (i, 0)),
        compiler_params=_params("parallel"),
        name="cast_rows",
    )(w)


def cast_col_groups(w, layer, dtype, groups, tn, cols_out):
    _, r, gf = w.shape
    f = gf // groups
    n_valid = f // tn
    n_out = cols_out // tn
    return pl.pallas_call(
        functools.partial(_cast_kernel, axis=1, n_valid=n_valid),
        out_shape=jax.ShapeDtypeStruct((r, groups * cols_out), dtype),
        grid=(groups, n_out),
        in_specs=[pl.BlockSpec((None, r, tn),
                               lambda g, j: (layer, 0, g * n_valid + jnp.minimum(j, n_valid - 1)))],
        out_specs=pl.BlockSpec((r, tn), lambda g, j: (0, g * n_out + j)),
        compiler_params=_params("parallel", "parallel"),
        name="cast_col_groups",
    )(w)


def _matmul_kernel(x_ref, w_ref, o_ref, *, scaled_blocks, scale):
    acc = _dot(x_ref[...], w_ref[...].astype(BF16))
    if scaled_blocks:
        acc = acc * jnp.where(pl.program_id(1) < scaled_blocks, scale, 1.0)
    o_ref[...] = acc.astype(o_ref.dtype)


def matmul(x, w, layer, out_dtype, tm=2048, tn=512, scaled_cols=0, scale=1.0):
    m, k = x.shape
    n = w.shape[2]
    return pl.pallas_call(
        functools.partial(_matmul_kernel, scaled_blocks=scaled_cols // tn, scale=scale),
        out_shape=jax.ShapeDtypeStruct((m, n), out_dtype),
        grid=(m // tm, n // tn),
        in_specs=[pl.BlockSpec((tm, k), lambda i, j: (i, 0), pipeline_mode=pl.Buffered(1)),
                  pl.BlockSpec((None, k, tn), lambda i, j: (layer, 0, j))],
        out_specs=pl.BlockSpec((tm, tn), lambda i, j: (i, j)),
        compiler_params=_params("parallel", "arbitrary"),
        name="matmul",
    )(x, w)


def _matmul_residual_kernel(x_ref, w_ref, h_ref, o_ref, *acc, scale, nk):
    if nk == 1:
        o_ref[...] = h_ref[...] + scale * _dot(x_ref[...], w_ref[...])
        return
    acc_s, = acc
    kk = pl.program_id(2)

    @pl.when(kk == 0)
    def _():
        acc_s[...] = _dot(x_ref[...], w_ref[...])

    @pl.when(jnp.logical_and(kk > 0, kk < nk - 1))
    def _():
        acc_s[...] += _dot(x_ref[...], w_ref[...])

    @pl.when(kk == nk - 1)
    def _():
        o_ref[...] = h_ref[...] + scale * (acc_s[...] + _dot(x_ref[...], w_ref[...]))


def matmul_residual(x, w, h, scale, tm=1024, tn=1024, tk=None):
    m, k = x.shape
    n = w.shape[1]
    tk = k if tk is None else tk
    nk = k // tk
    return pl.pallas_call(
        functools.partial(_matmul_residual_kernel, scale=scale, nk=nk),
        out_shape=jax.ShapeDtypeStruct((m, n), F32),
        grid=(m // tm, n // tn, nk),
        in_specs=[pl.BlockSpec((tm, tk), lambda i, j, kk: (i, kk)),
                  pl.BlockSpec((tk, tn), lambda i, j, kk: (kk, j)),
                  pl.BlockSpec((tm, tn), lambda i, j, kk: (i, j))],
        out_specs=pl.BlockSpec((tm, tn), lambda i, j, kk: (i, j)),
        scratch_shapes=[pltpu.VMEM((tm, tn), F32)] if nk > 1 else [],
        compiler_params=_params("parallel", "parallel", "arbitrary"),
        name="matmul_residual",
    )(x, w, h)


def _ffn_up_kernel(x_ref, wg_ref, wu_ref, o_ref, *, n_valid):
    j = pl.program_id(1)

    @pl.when(j < n_valid)
    def _():
        x = x_ref[...]
        g = _dot(x, wg_ref[...].astype(BF16))
        u = _dot(x, wu_ref[...].astype(BF16))
        o_ref[...] = (_silu(g) * u).astype(o_ref.dtype)

    @pl.when(j >= n_valid)
    def _():
        o_ref[...] = jnp.zeros_like(o_ref)


def ffn_up(x, w_gu, layer, f_out, tm=2048, tn=256):
    m, k = x.shape
    f = w_gu.shape[2] // 2
    n_valid = f // tn
    return pl.pallas_call(
        functools.partial(_ffn_up_kernel, n_valid=n_valid),
        out_shape=jax.ShapeDtypeStruct((m, f_out), BF16),
        grid=(m // tm, f_out // tn),
        in_specs=[pl.BlockSpec((tm, k), lambda i, j: (i, 0), pipeline_mode=pl.Buffered(1)),
                  pl.BlockSpec((None, k, tn), lambda i, j: (layer, 0, jnp.minimum(j, n_valid - 1))),
                  pl.BlockSpec((None, k, tn),
                               lambda i, j: (layer, 0, n_valid + jnp.minimum(j, n_valid - 1)))],
        out_specs=pl.BlockSpec((tm, tn), lambda i, j: (i, j)),
        compiler_params=_params("parallel", "arbitrary"),
        name="ffn_up",
    )(x, w_gu, w_gu)


def _split3(x):
    a = x.astype(BF16)
    r = x - a.astype(F32)
    b = r.astype(BF16)
    c = (r - b.astype(F32)).astype(BF16)
    return a, b, c


def _hgrn_kernel(q_ref, f_ref, i_ref, gate_ref, lbl_ref, gn_ref, u_ref, o_ref,
                 q_s, k_s, b_s, p_s, st_s, d_s, *, layer, tc):
    n_chunks = tc // CHUNK
    n_sub = CHUNK // SUB

    @pl.when(pl.program_id(1) == 0)
    def _():
        st_s[...] = jnp.zeros_like(st_s)

    lbl = lbl_ref[...]
    e = jnp.exp(lbl - jnp.max(lbl, axis=0, keepdims=True))
    lb = jnp.sum(e[:layer + 1], axis=0, keepdims=True) / jnp.sum(e, axis=0, keepdims=True)

    row = lax.broadcasted_iota(jnp.int32, (CHUNK, CHUNK), 0)
    col = lax.broadcasted_iota(jnp.int32, (CHUNK, CHUNK), 1)
    tril = jnp.where(col <= row, 1.0, 0.0).astype(BF16)
    tril3 = jnp.concatenate([tril, tril, tril], axis=1)
    diag_mask = ((row // SUB) == (col // SUB)) & (col <= row)
    off_mask = (col // SUB) < (row // SUB)


    for c in range(n_chunks):
        rows = slice(c * CHUNK, (c + 1) * CHUNK)
        q = _silu(q_ref[rows, :])
        f = lb + (1.0 - lb) * _sigmoid(f_ref[rows, :])
        k = 1.0 - f
        g1, g2, g3 = _split3(jnp.log2(f))
        b = _dot(tril3, jnp.concatenate([g1, g2, g3], axis=0))
        q_s[rows, :] = q
        k_s[rows, :] = k
        b_s[rows, :] = b
        for i in range(n_sub):
            r0 = c * CHUNK + i * SUB
            qi = q[i * SUB:(i + 1) * SUB]
            bi = b[i * SUB:(i + 1) * SUB]
            for s in range(SUB):
                r = i * SUB + s
                p = qi * k[r:r + 1] * jnp.exp2(jnp.minimum(bi - b[r:r + 1], 0.0))
                p_s[r0:r0 + SUB, s * HEAD_DIM:(s + 1) * HEAD_DIM] = p.astype(BF16)

    d_s[...] = _dot(p_s[...], u_ref[...])

    a_off, kv, st_decay = [], [], []
    for c in range(n_chunks):
        rows = slice(c * CHUNK, (c + 1) * CHUNK)
        q = q_s[rows, :]
        k = k_s[rows, :]
        b = b_s[rows, :]
        blocks = [jnp.zeros((SUB, CHUNK), F32)]
        for i in range(1, n_sub):
            bref = b[i * SUB - 1:i * SUB]
            qs = q[i * SUB:(i + 1) * SUB] * jnp.exp2(b[i * SUB:(i + 1) * SUB] - bref)
            ks = k * jnp.exp2(jnp.minimum(bref - b, 0.0))
            blocks.append(_dot_nt(qs.astype(BF16), ks.astype(BF16)))
        a_off.append(jnp.concatenate(blocks, axis=0))
        b_last = b[CHUNK - 1:CHUNK]
        vb = i_ref[rows, :].astype(BF16)
        kv.append(_dot_tn(vb, (k * jnp.exp2(b_last - b)).astype(BF16)))
        st_decay.append(jnp.exp2(b_last))

    o_intra = []
    for c in range(n_chunks):
        rows = slice(c * CHUNK, (c + 1) * CHUNK)
        d = d_s[rows, :][:, :CHUNK]
        a = jnp.where(diag_mask, d, jnp.where(off_mask, a_off[c], 0.0))
        o_intra.append(_dot(a.astype(BF16), i_ref[rows, :].astype(BF16)))

    gn = gn_ref[...]
    st = st_s[...]
    for c in range(n_chunks):
        rows = slice(c * CHUNK, (c + 1) * CHUNK)
        qb = (q_s[rows, :] * jnp.exp2(b_s[rows, :])).astype(BF16)
        o = _dot_nt(qb, st.astype(BF16)) + o_intra[c]
        st = st * st_decay[c] + kv[c]
        y = o * lax.rsqrt(jnp.mean(o * o, axis=-1, keepdims=True) + EPS) * gn
        o_ref[rows, :] = (y * _silu(gate_ref[rows, :])).astype(o_ref.dtype)
    st_s[...] = st


def hgrn_core(proj, lb_logits, gnorm, layer, tc=512):
    s, d4 = proj.shape
    d = d4 // 4
    heads = d // HEAD_DIM
    n_l = lb_logits.shape[0]
    r = jnp.arange(SUB * HEAD_DIM)[:, None] // HEAD_DIM
    c = jnp.arange(HEAD_DIM)[None, :] % SUB
    u = (r == c).astype(BF16)

    def col(off):
        return pl.BlockSpec((tc, HEAD_DIM), lambda h, t: (t, off * heads + h))

    return pl.pallas_call(
        functools.partial(_hgrn_kernel, layer=layer, tc=tc),
        out_shape=jax.ShapeDtypeStruct((s, d), BF16),
        grid=(heads, s // tc),
        in_specs=[col(0), col(1), col(2), col(3),
                  pl.BlockSpec((n_l, HEAD_DIM), lambda h, t: (0, h)),
                  pl.BlockSpec((1, HEAD_DIM), lambda h, t: (0, 0)),
                  pl.BlockSpec((SUB * HEAD_DIM, HEAD_DIM), lambda h, t: (0, 0))],
        out_specs=pl.BlockSpec((tc, HEAD_DIM), lambda h, t: (t, h)),
        scratch_shapes=[pltpu.VMEM((tc, HEAD_DIM), F32),
                        pltpu.VMEM((tc, HEAD_DIM), F32),
                        pltpu.VMEM((tc, HEAD_DIM), F32),
                        pltpu.VMEM((tc, SUB * HEAD_DIM), BF16),
                        pltpu.VMEM((HEAD_DIM, HEAD_DIM), F32),
                        pltpu.VMEM((tc, HEAD_DIM), F32)],
        compiler_params=_params("parallel", "arbitrary"),
        name="hgrn_core",
    )(proj, proj, proj, proj, lb_logits, gnorm.reshape(1, HEAD_DIM), u)


def _sb_kernel(q_ref, k_ref, v_ref, w_ref, o_ref, acc_s, c_s, *, tq, sk, hpb):
    qi = pl.program_id(1)
    w = w_ref[...]
    acc_s[...] = jnp.zeros_like(acc_s)
    c_s[...] = jnp.zeros_like(c_s)
    n_sub = tq // sk

    def tile(k0, mask):
        for hh in range(hpb):
            hc = slice(hh * HEAD_DIM, (hh + 1) * HEAD_DIM)
            kb = k_ref[pl.ds(k0, tq), hc]
            vb = v_ref[pl.ds(k0, tq), hc]
            z = _dot_nt(q_ref[:, hc], kb)
            nz = -z
            lk = jnp.minimum(nz, 0.0) - jnp.log2(1.0 + jnp.exp2(jnp.minimum(z, nz)))
            if mask is not None:
                lk = jnp.where(mask, lk, 0.0)
            lkb = lk.astype(BF16)
            c = c_s[hh]
            parts = []
            for s in range(n_sub - 1, -1, -1):
                cols = slice(s * sk, (s + 1) * sk)
                r = _dot(lkb[:, cols], w)
                cc = jnp.concatenate([c] * (sk // HEAD_DIM), axis=1)
                parts.append(jnp.exp2(z[:, cols] + r + cc))
                c = c + jnp.broadcast_to(r[:, 0:1], c.shape)
            a = jnp.concatenate(parts[::-1], axis=1)
            if mask is not None:
                a = jnp.where(mask, a, 0.0)
            acc_s[hh] += _dot(a.astype(BF16), vb)
            c_s[hh] = c

    row = lax.broadcasted_iota(jnp.int32, (tq, tq), 0)
    col = lax.broadcasted_iota(jnp.int32, (tq, tq), 1)
    tile(pl.multiple_of(qi * tq, tq), col < row)

    def cond(carry):
        j, c_max = carry
        return jnp.logical_and(j < qi, c_max > SB_UNDERFLOW_LOG2)

    def body(carry):
        j, _ = carry
        tile(pl.multiple_of((qi - 1 - j) * tq, tq), None)
        return j + 1, jnp.max(c_s[...])

    lax.while_loop(cond, body, (jnp.int32(0), jnp.max(c_s[...])))
    for hh in range(hpb):
        o_ref[:, hh * HEAD_DIM:(hh + 1) * HEAD_DIM] = acc_s[hh].astype(o_ref.dtype)


def sb_core(qkv, tq=512, sk=256, hpb=2):
    s, d3 = qkv.shape
    d = d3 // 3
    groups = d // (hpb * HEAD_DIM)
    j = jnp.arange(sk)
    w = (j[:, None] >= j[None, :]).astype(BF16)
    kern = functools.partial(_sb_kernel, tq=tq, sk=sk, hpb=hpb)
    wide = hpb * HEAD_DIM
    return pl.pallas_call(
        kern,
        out_shape=jax.ShapeDtypeStruct((s, d), BF16),
        grid=(groups, s // tq),
        in_specs=[pl.BlockSpec((tq, wide), lambda h, i: (i, h)),
                  pl.BlockSpec((s, wide), lambda h, i: (0, groups + h)),
                  pl.BlockSpec((s, wide), lambda h, i: (0, 2 * groups + h)),
                  pl.BlockSpec((sk, sk), lambda h, i: (0, 0))],
        out_specs=pl.BlockSpec((tq, wide), lambda h, i: (i, h)),
        scratch_shapes=[pltpu.VMEM((hpb, tq, HEAD_DIM), F32),
                        pltpu.VMEM((hpb, tq, HEAD_DIM), F32)],
        compiler_params=_params("parallel", "arbitrary"),
        name="sb_core",
    )(qkv, qkv, qkv, w)


def _ffn_half_step(h, norm_w, w_gu, w_down, layer):
    f = w_down.shape[1]
    fp = -(-f // FF_PAD) * FF_PAD
    down = cast_rows(w_down, layer, BF16, tr=FF_TILE, rows_out=fp)
    u = rmsnorm(h, norm_w[layer], BF16)
    a = ffn_up(u, w_gu, layer, fp, tn=FF_TILE)
    return matmul_residual(a, down, h, 0.5, tk=fp // 4)


def _cast_weight(w, layer):
    n = w.shape[2]
    tr = 8
    while 2 * tr * n <= 2 * 1024 * 1024 and w.shape[1] % (2 * tr) == 0:
        tr *= 2
    return cast_rows(w, layer, BF16, tr=tr)


def kernel(x, ffn1_norm, ffn1_w_gu, ffn1_w_down, mix_norm, ffn2_norm, ffn2_w_gu, ffn2_w_down,
           hgrn_w_in, hgrn_lb_logits, hgrn_gnorm, hgrn_w_out, sb_w_in, sb_w_out, final_norm):
    b, s, d = x.shape
    depth = ffn1_norm.shape[0]
    outs = []
    for bi in range(b):
        h = x[bi]
        for i in range(depth):
            h = _ffn_half_step(h, ffn1_norm, ffn1_w_gu, ffn1_w_down, i)
            u = rmsnorm(h, mix_norm[i], BF16)
            j = i // 2
            if i % 2 == 0:
                proj = matmul(u, hgrn_w_in, j, F32)
                o = hgrn_core(proj, hgrn_lb_logits, hgrn_gnorm[j], layer=i)
                h = matmul_residual(o, _cast_weight(hgrn_w_out, j), h, 1.0)
            else:
                qkv = matmul(u, sb_w_in, j, BF16, scaled_cols=d,
                             scale=math.log2(math.e) / math.sqrt(HEAD_DIM))
                o = sb_core(qkv)
                h = matmul_residual(o, _cast_weight(sb_w_out, j), h, 1.0)
            h = _ffn_half_step(h, ffn2_norm, ffn2_w_gu, ffn2_w_down, i)
        outs.append(rmsnorm(h, final_norm, x.dtype))
    return jnp.stack(outs, axis=0)
```

```python
import functools
import math

import jax
import jax.numpy as jnp
from jax import lax
from jax.experimental import pallas as pl
from jax.experimental.pallas import tpu as pltpu

EPS = 1e-6
CHUNK = 64
SUB = 8
HEAD_DIM = 128
FF_PAD = 1024
FF_TILE = 256
F32_MIN_SUBNORMAL_LOG2 = -149.0
SB_UNDERFLOW_LOG2 = F32_MIN_SUBNORMAL_LOG2 - 43.0

VMEM_LIMIT_BYTES = 56 * 1024 * 1024

BF16 = jnp.bfloat16
F32 = jnp.float32


def _params(*sem):
    return pltpu.CompilerParams(dimension_semantics=sem, vmem_limit_bytes=VMEM_LIMIT_BYTES)


def _dot(a, b):
    return jnp.dot(a, b, preferred_element_type=F32)


def _dot_nt(a, b):
    return lax.dot_general(a, b, (((1,), (1,)), ((), ())), preferred_element_type=F32)


def _dot_tn(a, b):
    return lax.dot_general(a, b, (((0,), (0,)), ((), ())), preferred_element_type=F32)


def _sigmoid(x):
    return 1.0 / (1.0 + jnp.exp(-x))


def _silu(x):
    return x * _sigmoid(x)


def _rmsnorm_kernel(x_ref, w_ref, o_ref):
    x = x_ref[...]
    y = x * lax.rsqrt(jnp.mean(x * x, axis=-1, keepdims=True) + EPS)
    o_ref[...] = (y * w_ref[...]).astype(o_ref.dtype)


def rmsnorm(x, w, out_dtype, tm=256):
    s, d = x.shape
    return pl.pallas_call(
        _rmsnorm_kernel,
        out_shape=jax.ShapeDtypeStruct((s, d), out_dtype),
        grid=(s // tm,),
        in_specs=[pl.BlockSpec((tm, d), lambda i: (i, 0)),
                  pl.BlockSpec((1, d), lambda i: (0, 0))],
        out_specs=pl.BlockSpec((tm, d), lambda i: (i, 0)),
        compiler_params=_params("parallel"),
        name="rmsnorm",
    )(x, w.reshape(1, d))


def _cast_kernel(x_ref, o_ref, *, axis, n_valid):
    x = x_ref[...].astype(o_ref.dtype)
    if n_valid is not None:
        x = jnp.where(pl.program_id(axis) < n_valid, x, jnp.zeros_like(x))
    o_ref[...] = x


def cast_rows(w, layer, dtype, tr, rows_out=None):
    _, r, n = w.shape
    rows_out = r if rows_out is None else rows_out
    n_valid = r // tr
    return pl.pallas_call(
        functools.partial(_cast_kernel, axis=0, n_valid=None if rows_out == r else n_valid),
        out_shape=jax.ShapeDtypeStruct((rows_out, n), dtype),
        grid=(rows_out // tr,),
        in_specs=[pl.BlockSpec((None, tr, n), lambda i: (layer, jnp.minimum(i, n_valid - 1), 0))],
        out_specs=pl.BlockSpec((tr, n), lambda i: (i, 0)),
        compiler_params=_params("parallel"),
        name="cast_rows",
    )(w)


def cast_col_groups(w, layer, dtype, groups, tn, cols_out):
    _, r, gf = w.shape
    f = gf // groups
    n_valid = f // tn
    n_out = cols_out // tn
    return pl.pallas_call(
        functools.partial(_cast_kernel, axis=1, n_valid=n_valid),
        out_shape=jax.ShapeDtypeStruct((r, groups * cols_out), dtype),
        grid=(groups, n_out),
        in_specs=[pl.BlockSpec((None, r, tn),
                               lambda g, j: (layer, 0, g * n_valid + jnp.minimum(j, n_valid - 1)))],
        out_specs=pl.BlockSpec((r, tn), lambda g, j: (0, g * n_out + j)),
        compiler_params=_params("parallel", "parallel"),
        name="cast_col_groups",
    )(w)


def _matmul_kernel(x_ref, w_ref, o_ref, *, scaled_blocks, scale):
    acc = _dot(x_ref[...], w_ref[...].astype(BF16))
    if scaled_blocks:
        acc = acc * jnp.where(pl.program_id(1) < scaled_blocks, scale, 1.0)
    o_ref[...] = acc.astype(o_ref.dtype)


def matmul(x, w, layer, out_dtype, tm=2048, tn=512, scaled_cols=0, scale=1.0):
    m, k = x.shape
    n = w.shape[2]
    return pl.pallas_call(
        functools.partial(_matmul_kernel, scaled_blocks=scaled_cols // tn, scale=scale),
        out_shape=jax.ShapeDtypeStruct((m, n), out_dtype),
        grid=(m // tm, n // tn),
        in_specs=[pl.BlockSpec((tm, k), lambda i, j: (i, 0), pipeline_mode=pl.Buffered(1)),
                  pl.BlockSpec((None, k, tn), lambda i, j: (layer, 0, j))],
        out_specs=pl.BlockSpec((tm, tn), lambda i, j: (i, j)),
        compiler_params=_params("parallel", "arbitrary"),
        name="matmul",
    )(x, w)


def _matmul_residual_kernel(x_ref, w_ref, h_ref, o_ref, *acc, scale, nk):
    if nk == 1:
        o_ref[...] = h_ref[...] + scale * _dot(x_ref[...], w_ref[...])
        return
    acc_s, = acc
    kk = pl.program_id(2)

    @pl.when(kk == 0)
    def _():
        acc_s[...] = _dot(x_ref[...], w_ref[...])

    @pl.when(jnp.logical_and(kk > 0, kk < nk - 1))
    def _():
        acc_s[...] += _dot(x_ref[...], w_ref[...])

    @pl.when(kk == nk - 1)
    def _():
        o_ref[...] = h_ref[...] + scale * (acc_s[...] + _dot(x_ref[...], w_ref[...]))


def matmul_residual(x, w, h, scale, tm=1024, tn=1024, tk=None):
    m, k = x.shape
    n = w.shape[1]
    tk = k if tk is None else tk
    nk = k // tk
    return pl.pallas_call(
        functools.partial(_matmul_residual_kernel, scale=scale, nk=nk),
        out_shape=jax.ShapeDtypeStruct((m, n), F32),
        grid=(m // tm, n // tn, nk),
        in_specs=[pl.BlockSpec((tm, tk), lambda i, j, kk: (i, kk)),
                  pl.BlockSpec((tk, tn), lambda i, j, kk: (kk, j)),
                  pl.BlockSpec((tm, tn), lambda i, j, kk: (i, j))],
        out_specs=pl.BlockSpec((tm, tn), lambda i, j, kk: (i, j)),
        scratch_shapes=[pltpu.VMEM((tm, tn), F32)] if nk > 1 else [],
        compiler_params=_params("parallel", "parallel", "arbitrary"),
        name="matmul_residual",
    )(x, w, h)


def _ffn_up_kernel(x_ref, wg_ref, wu_ref, wd_ref, o_ref, d_ref, *, n_valid):
    j = pl.program_id(1)

    @pl.when(j < n_valid)
    def _():
        x = x_ref[...]
        g = _dot(x, wg_ref[...].astype(BF16))
        u = _dot(x, wu_ref[...].astype(BF16))
        o_ref[...] = (_silu(g) * u).astype(o_ref.dtype)
        d_ref[...] = wd_ref[...].astype(d_ref.dtype)

    @pl.when(j >= n_valid)
    def _():
        o_ref[...] = jnp.zeros_like(o_ref)
        d_ref[...] = jnp.zeros_like(d_ref)


def ffn_up(x, w_gu, w_down, layer, f_out, tm=2048, tn=256):
    m, k = x.shape
    f = w_gu.shape[2] // 2
    d = w_down.shape[2]
    n_valid = f // tn
    n_out = f_out // tn

    def w_block(j):
        return jnp.minimum(j, n_valid - 1)

    return pl.pallas_call(
        functools.partial(_ffn_up_kernel, n_valid=n_valid),
        out_shape=(jax.ShapeDtypeStruct((m, f_out), BF16),
                   jax.ShapeDtypeStruct((f_out + tn, d), BF16)),
        grid=(m // tm, n_out),
        in_specs=[pl.BlockSpec((tm, k), lambda i, j: (i, 0), pipeline_mode=pl.Buffered(1)),
                  pl.BlockSpec((None, k, tn), lambda i, j: (layer, 0, w_block(j))),
                  pl.BlockSpec((None, k, tn), lambda i, j: (layer, 0, n_valid + w_block(j))),
                  pl.BlockSpec((None, tn, d),
                               lambda i, j: (layer, jnp.where(i == 0, w_block(j), n_valid - 1), 0))],
        out_specs=(pl.BlockSpec((tm, tn), lambda i, j: (i, j)),
                   pl.BlockSpec((tn, d), lambda i, j: (jnp.where(i == 0, j, n_out), 0))),
        compiler_params=_params("arbitrary", "arbitrary"),
        name="ffn_up",
    )(x, w_gu, w_gu, w_down)


def _split3(x):
    a = x.astype(BF16)
    r = x - a.astype(F32)
    b = r.astype(BF16)
    c = (r - b.astype(F32)).astype(BF16)
    return a, b, c


def _hgrn_kernel(q_ref, f_ref, i_ref, gate_ref, lbl_ref, gn_ref, u_ref, o_ref,
                 q_s, k_s, b_s, p_s, st_s, d_s, *, layer, tc):
    n_chunks = tc // CHUNK
    n_sub = CHUNK // SUB

    @pl.when(pl.program_id(1) == 0)
    def _():
        st_s[...] = jnp.zeros_like(st_s)

    lbl = lbl_ref[...]
    e = jnp.exp(lbl - jnp.max(lbl, axis=0, keepdims=True))
    lb = jnp.sum(e[:layer + 1], axis=0, keepdims=True) / jnp.sum(e, axis=0, keepdims=True)

    row = lax.broadcasted_iota(jnp.int32, (CHUNK, CHUNK), 0)
    col = lax.broadcasted_iota(jnp.int32, (CHUNK, CHUNK), 1)
    tril = jnp.where(col <= row, 1.0, 0.0).astype(BF16)
    tril3 = jnp.concatenate([tril, tril, tril], axis=1)
    diag_mask = ((row // SUB) == (col // SUB)) & (col <= row)
    off_mask = (col // SUB) < (row // SUB)


    for c in range(n_chunks):
        rows = slice(c * CHUNK, (c + 1) * CHUNK)
        q = _silu(q_ref[rows, :])
        f = lb + (1.0 - lb) * _sigmoid(f_ref[rows, :])
        k = 1.0 - f
        g1, g2, g3 = _split3(jnp.log2(f))
        b = _dot(tril3, jnp.concatenate([g1, g2, g3], axis=0))
        q_s[rows, :] = q
        k_s[rows, :] = k
        b_s[rows, :] = b
        for i in range(n_sub):
            r0 = c * CHUNK + i * SUB
            qi = q[i * SUB:(i + 1) * SUB]
            bi = b[i * SUB:(i + 1) * SUB]
            for s in range(SUB):
                r = i * SUB + s
                p = qi * k[r:r + 1] * jnp.exp2(jnp.minimum(bi - b[r:r + 1], 0.0))
                p_s[r0:r0 + SUB, s * HEAD_DIM:(s + 1) * HEAD_DIM] = p.astype(BF16)

    d_s[...] = _dot(p_s[...], u_ref[...])

    a_off, kv, st_decay = [], [], []
    for c in range(n_chunks):
        rows = slice(c * CHUNK, (c + 1) * CHUNK)
        q = q_s[rows, :]
        k = k_s[rows, :]
        b = b_s[rows, :]
        blocks = [jnp.zeros((SUB, CHUNK), F32)]
        for i in range(1, n_sub):
            bref = b[i * SUB - 1:i * SUB]
            qs = q[i * SUB:(i + 1) * SUB] * jnp.exp2(b[i * SUB:(i + 1) * SUB] - bref)
            ks = k * jnp.exp2(jnp.minimum(bref - b, 0.0))
            blocks.append(_dot_nt(qs.astype(BF16), ks.astype(BF16)))
        a_off.append(jnp.concatenate(blocks, axis=0))
        b_last = b[CHUNK - 1:CHUNK]
        vb = i_ref[rows, :].astype(BF16)
        kv.append(_dot_tn(vb, (k * jnp.exp2(b_last - b)).astype(BF16)))
        st_decay.append(jnp.exp2(b_last))

    o_intra = []
    for c in range(n_chunks):
        rows = slice(c * CHUNK, (c + 1) * CHUNK)
        d = d_s[rows, :][:, :CHUNK]
        a = jnp.where(diag_mask, d, jnp.where(off_mask, a_off[c], 0.0))
        o_intra.append(_dot(a.astype(BF16), i_ref[rows, :].astype(BF16)))

    gn = gn_ref[...]
    st = st_s[...]
    for c in range(n_chunks):
        rows = slice(c * CHUNK, (c + 1) * CHUNK)
        qb = (q_s[rows, :] * jnp.exp2(b_s[rows, :])).astype(BF16)
        o = _dot_nt(qb, st.astype(BF16)) + o_intra[c]
        st = st * st_decay[c] + kv[c]
        y = o * lax.rsqrt(jnp.mean(o * o, axis=-1, keepdims=True) + EPS) * gn
        o_ref[rows, :] = (y * _silu(gate_ref[rows, :])).astype(o_ref.dtype)
    st_s[...] = st


def hgrn_core(proj, lb_logits, gnorm, layer, tc=512):
    s, d4 = proj.shape
    d = d4 // 4
    heads = d // HEAD_DIM
    n_l = lb_logits.shape[0]
    r = jnp.arange(SUB * HEAD_DIM)[:, None] // HEAD_DIM
    c = jnp.arange(HEAD_DIM)[None, :] % SUB
    u = (r == c).astype(BF16)

    def col(off):
        return pl.BlockSpec((tc, HEAD_DIM), lambda h, t: (t, off * heads + h))

    return pl.pallas_call(
        functools.partial(_hgrn_kernel, layer=layer, tc=tc),
        out_shape=jax.ShapeDtypeStruct((s, d), BF16),
        grid=(heads, s // tc),
        in_specs=[col(0), col(1), col(2), col(3),
                  pl.BlockSpec((n_l, HEAD_DIM), lambda h, t: (0, h)),
                  pl.BlockSpec((1, HEAD_DIM), lambda h, t: (0, 0)),
                  pl.BlockSpec((SUB * HEAD_DIM, HEAD_DIM), lambda h, t: (0, 0))],
        out_specs=pl.BlockSpec((tc, HEAD_DIM), lambda h, t: (t, h)),
        scratch_shapes=[pltpu.VMEM((tc, HEAD_DIM), F32),
                        pltpu.VMEM((tc, HEAD_DIM), F32),
                        pltpu.VMEM((tc, HEAD_DIM), F32),
                        pltpu.VMEM((tc, SUB * HEAD_DIM), BF16),
                        pltpu.VMEM((HEAD_DIM, HEAD_DIM), F32),
                        pltpu.VMEM((tc, HEAD_DIM), F32)],
        compiler_params=_params("parallel", "arbitrary"),
        name="hgrn_core",
    )(proj, proj, proj, proj, lb_logits, gnorm.reshape(1, HEAD_DIM), u)


def _sb_kernel(q_ref, k_ref, v_ref, w_ref, o_ref, acc_s, c_s, *, tq, sk, hpb):
    qi = pl.program_id(1)
    w = w_ref[...]
    acc_s[...] = jnp.zeros_like(acc_s)
    c_s[...] = jnp.zeros_like(c_s)
    n_sub = tq // sk

    def tile(k0, mask):
        for hh in range(hpb):
            hc = slice(hh * HEAD_DIM, (hh + 1) * HEAD_DIM)
            kb = k_ref[pl.ds(k0, tq), hc]
            vb = v_ref[pl.ds(k0, tq), hc]
            z = _dot_nt(q_ref[:, hc], kb)
            nz = -z
            lk = jnp.minimum(nz, 0.0) - jnp.log2(1.0 + jnp.exp2(jnp.minimum(z, nz)))
            if mask is not None:
                lk = jnp.where(mask, lk, 0.0)
            lkb = lk.astype(BF16)
            c = c_s[hh]
            parts = []
            for s in range(n_sub - 1, -1, -1):
                cols = slice(s * sk, (s + 1) * sk)
                r = _dot(lkb[:, cols], w)
                cc = jnp.concatenate([c] * (sk // HEAD_DIM), axis=1)
                parts.append(jnp.exp2(z[:, cols] + r + cc))
                c = c + jnp.broadcast_to(r[:, 0:1], c.shape)
            a = jnp.concatenate(parts[::-1], axis=1)
            if mask is not None:
                a = jnp.where(mask, a, 0.0)
            acc_s[hh] += _dot(a.astype(BF16), vb)
            c_s[hh] = c

    row = lax.broadcasted_iota(jnp.int32, (tq, tq), 0)
    col = lax.broadcasted_iota(jnp.int32, (tq, tq), 1)
    tile(pl.multiple_of(qi * tq, tq), col < row)

    def cond(carry):
        j, c_max = carry
        return jnp.logical_and(j < qi, c_max > SB_UNDERFLOW_LOG2)

    def body(carry):
        j, _ = carry
        tile(pl.multiple_of((qi - 1 - j) * tq, tq), None)
        return j + 1, jnp.max(c_s[...])

    lax.while_loop(cond, body, (jnp.int32(0), jnp.max(c_s[...])))
    for hh in range(hpb):
        o_ref[:, hh * HEAD_DIM:(hh + 1) * HEAD_DIM] = acc_s[hh].astype(o_ref.dtype)


def sb_core(qkv, tq=256, sk=256, hpb=4):
    s, d3 = qkv.shape
    d = d3 // 3
    groups = d // (hpb * HEAD_DIM)
    j = jnp.arange(sk)
    w = (j[:, None] >= j[None, :]).astype(BF16)
    kern = functools.partial(_sb_kernel, tq=tq, sk=sk, hpb=hpb)
    wide = hpb * HEAD_DIM
    return pl.pallas_call(
        kern,
        out_shape=jax.ShapeDtypeStruct((s, d), BF16),
        grid=(groups, s // tq),
        in_specs=[pl.BlockSpec((tq, wide), lambda h, i: (i, h)),
                  pl.BlockSpec((s, wide), lambda h, i: (0, groups + h)),
                  pl.BlockSpec((s, wide), lambda h, i: (0, 2 * groups + h)),
                  pl.BlockSpec((sk, sk), lambda h, i: (0, 0))],
        out_specs=pl.BlockSpec((tq, wide), lambda h, i: (i, h)),
        scratch_shapes=[pltpu.VMEM((hpb, tq, HEAD_DIM), F32),
                        pltpu.VMEM((hpb, tq, HEAD_DIM), F32)],
        compiler_params=_params("parallel", "arbitrary"),
        name="sb_core",
    )(qkv, qkv, qkv, w)


def _ffn_half_step(h, norm_w, w_gu, w_down, layer):
    f = w_down.shape[1]
    fp = -(-f // FF_PAD) * FF_PAD
    u = rmsnorm(h, norm_w[layer], BF16)
    a, down = ffn_up(u, w_gu, w_down, layer, fp, tn=FF_TILE)
    return matmul_residual(a, down, h, 0.5, tk=fp // 4)


def _cast_weight(w, layer):
    n = w.shape[2]
    tr = 8
    while 2 * tr * n <= 2 * 1024 * 1024 and w.shape[1] % (2 * tr) == 0:
        tr *= 2
    return cast_rows(w, layer, BF16, tr=tr)


def kernel(x, ffn1_norm, ffn1_w_gu, ffn1_w_down, mix_norm, ffn2_norm, ffn2_w_gu, ffn2_w_down,
           hgrn_w_in, hgrn_lb_logits, hgrn_gnorm, hgrn_w_out, sb_w_in, sb_w_out, final_norm):
    b, s, d = x.shape
    depth = ffn1_norm.shape[0]
    outs = []
    for bi in range(b):
        h = x[bi]
        for i in range(depth):
            h = _ffn_half_step(h, ffn1_norm, ffn1_w_gu, ffn1_w_down, i)
            u = rmsnorm(h, mix_norm[i], BF16)
            j = i // 2
            if i % 2 == 0:
                proj = matmul(u, hgrn_w_in, j, F32)
                o = hgrn_core(proj, hgrn_lb_logits, hgrn_gnorm[j], layer=i)
                h = matmul_residual(o, _cast_weight(hgrn_w_out, j), h, 1.0)
            else:
                qkv = matmul(u, sb_w_in, j, BF16, scaled_cols=d,
                             scale=math.log2(math.e) / math.sqrt(HEAD_DIM))
                o = sb_core(qkv)
                h = matmul_residual(o, _cast_weight(sb_w_out, j), h, 1.0)
            h = _ffn_half_step(h, ffn2_norm, ffn2_w_gu, ffn2_w_down, i)
        outs.append(rmsnorm(h, final_norm, x.dtype))
    return jnp.stack(outs, axis=0)
```

```python
import functools
import math

import jax
import jax.numpy as jnp
from jax import lax
from jax.experimental import pallas as pl
from jax.experimental.pallas import tpu as pltpu

EPS = 1e-6
CHUNK = 64
SUB = 8
HEAD_DIM = 128
FF_PAD = 1024
FF_TILE = 256
FFN_UP_ROWS = 1024
MATMUL_ROWS = 1024
F32_MIN_SUBNORMAL_LOG2 = -149.0
SB_UNDERFLOW_LOG2 = F32_MIN_SUBNORMAL_LOG2 - 43.0

VMEM_LIMIT_BYTES = 56 * 1024 * 1024

BF16 = jnp.bfloat16
F32 = jnp.float32


def _params(*sem):
    return pltpu.CompilerParams(dimension_semantics=sem, vmem_limit_bytes=VMEM_LIMIT_BYTES)


def _dot(a, b):
    return jnp.dot(a, b, preferred_element_type=F32)


def _dot_nt(a, b):
    return lax.dot_general(a, b, (((1,), (1,)), ((), ())), preferred_element_type=F32)


def _dot_tn(a, b):
    return lax.dot_general(a, b, (((0,), (0,)), ((), ())), preferred_element_type=F32)


def _sigmoid(x):
    return 1.0 / (1.0 + jnp.exp(-x))


def _silu(x):
    return x * _sigmoid(x)


def _rmsnorm_kernel(x_ref, w_ref, o_ref):
    x = x_ref[...]
    y = x * lax.rsqrt(jnp.mean(x * x, axis=-1, keepdims=True) + EPS)
    o_ref[...] = (y * w_ref[...]).astype(o_ref.dtype)


def rmsnorm(x, w, out_dtype, tm=256):
    s, d = x.shape
    return pl.pallas_call(
        _rmsnorm_kernel,
        out_shape=jax.ShapeDtypeStruct((s, d), out_dtype),
        grid=(s // tm,),
        in_specs=[pl.BlockSpec((tm, d), lambda i: (i, 0)),
                  pl.BlockSpec((1, d), lambda i: (0, 0))],
        out_specs=pl.BlockSpec((tm, d), lambda i: (i, 0)),
        compiler_params=_params("parallel"),
        name="rmsnorm",
    )(x, w.reshape(1, d))


def _cast_kernel(x_ref, o_ref, *, axis, n_valid):
    x = x_ref[...].astype(o_ref.dtype)
    if n_valid is not None:
        x = jnp.where(pl.program_id(axis) < n_valid, x, jnp.zeros_like(x))
    o_ref[...] = x


def cast_rows(w, layer, dtype, tr, rows_out=None):
    _, r, n = w.shape
    rows_out = r if rows_out is None else rows_out
    n_valid = r // tr
    return pl.pallas_call(
        functools.partial(_cast_kernel, axis=0, n_valid=None if rows_out == r else n_valid),
        out_shape=jax.ShapeDtypeStruct((rows_out, n), dtype),
        grid=(rows_out // tr,),
        in_specs=[pl.BlockSpec((None, tr, n), lambda i: (layer, jnp.minimum(i, n_valid - 1), 0))],
        out_specs=pl.BlockSpec((tr, n), lambda i: (i, 0)),
        compiler_params=_params("parallel"),
        name="cast_rows",
    )(w)


def cast_col_groups(w, layer, dtype, groups, tn, cols_out):
    _, r, gf = w.shape
    f = gf // groups
    n_valid = f // tn
    n_out = cols_out // tn
    return pl.pallas_call(
        functools.partial(_cast_kernel, axis=1, n_valid=n_valid),
        out_shape=jax.ShapeDtypeStruct((r, groups * cols_out), dtype),
        grid=(groups, n_out),
        in_specs=[pl.BlockSpec((None, r, tn),
                               lambda g, j: (layer, 0, g * n_valid + jnp.minimum(j, n_valid - 1)))],
        out_specs=pl.BlockSpec((r, tn), lambda g, j: (0, g * n_out + j)),
        compiler_params=_params("parallel", "parallel"),
        name="cast_col_groups",
    )(w)


def _matmul_kernel(x_ref, w_ref, o_ref, *, scaled_blocks, scale):
    w = w_ref[...].astype(BF16)
    for r in range(0, x_ref.shape[0], MATMUL_ROWS):
        rows = slice(r, r + MATMUL_ROWS)
        acc = _dot(x_ref[rows, :], w)
        if scaled_blocks:
            acc = acc * jnp.where(pl.program_id(1) < scaled_blocks, scale, 1.0)
        o_ref[rows, :] = acc.astype(o_ref.dtype)


def matmul(x, w, layer, out_dtype, tm=2048, tn=512, scaled_cols=0, scale=1.0):
    m, k = x.shape
    n = w.shape[2]
    return pl.pallas_call(
        functools.partial(_matmul_kernel, scaled_blocks=scaled_cols // tn, scale=scale),
        out_shape=jax.ShapeDtypeStruct((m, n), out_dtype),
        grid=(m // tm, n // tn),
        in_specs=[pl.BlockSpec((tm, k), lambda i, j: (i, 0), pipeline_mode=pl.Buffered(1)),
                  pl.BlockSpec((None, k, tn), lambda i, j: (layer, 0, j))],
        out_specs=pl.BlockSpec((tm, tn), lambda i, j: (i, j)),
        compiler_params=_params("parallel", "arbitrary"),
        name="matmul",
    )(x, w)


def _matmul_residual_kernel(x_ref, w_ref, h_ref, o_ref, *acc, scale, nk):
    if nk == 1:
        o_ref[...] = h_ref[...] + scale * _dot(x_ref[...], w_ref[...])
        return
    acc_s, = acc
    kk = pl.program_id(2)

    @pl.when(kk == 0)
    def _():
        acc_s[...] = _dot(x_ref[...], w_ref[...])

    @pl.when(jnp.logical_and(kk > 0, kk < nk - 1))
    def _():
        acc_s[...] += _dot(x_ref[...], w_ref[...])

    @pl.when(kk == nk - 1)
    def _():
        o_ref[...] = h_ref[...] + scale * (acc_s[...] + _dot(x_ref[...], w_ref[...]))


def matmul_residual(x, w, h, scale, tm=1024, tn=1024, tk=None):
    m, k = x.shape
    n = w.shape[1]
    tk = k if tk is None else tk
    nk = k // tk
    return pl.pallas_call(
        functools.partial(_matmul_residual_kernel, scale=scale, nk=nk),
        out_shape=jax.ShapeDtypeStruct((m, n), F32),
        grid=(m // tm, n // tn, nk),
        in_specs=[pl.BlockSpec((tm, tk), lambda i, j, kk: (i, kk)),
                  pl.BlockSpec((tk, tn), lambda i, j, kk: (kk, j)),
                  pl.BlockSpec((tm, tn), lambda i, j, kk: (i, j))],
        out_specs=pl.BlockSpec((tm, tn), lambda i, j, kk: (i, j)),
        scratch_shapes=[pltpu.VMEM((tm, tn), F32)] if nk > 1 else [],
        compiler_params=_params("parallel", "parallel", "arbitrary"),
        name="matmul_residual",
    )(x, w, h)


def _ffn_up_kernel(x_ref, wg_ref, wu_ref, wd_ref, o_ref, d_ref, *, n_valid):
    j = pl.program_id(1)

    @pl.when(j < n_valid)
    def _():
        wg = wg_ref[...].astype(BF16)
        wu = wu_ref[...].astype(BF16)
        for r in range(0, x_ref.shape[0], FFN_UP_ROWS):
            rows = slice(r, r + FFN_UP_ROWS)
            x = x_ref[rows, :]
            o_ref[rows, :] = (_silu(_dot(x, wg)) * _dot(x, wu)).astype(o_ref.dtype)
        d_ref[...] = wd_ref[...].astype(d_ref.dtype)

    @pl.when(j >= n_valid)
    def _():
        o_ref[...] = jnp.zeros_like(o_ref)
        d_ref[...] = jnp.zeros_like(d_ref)


def ffn_up(x, w_gu, w_down, layer, f_out, tm=2048, tn=256):
    m, k = x.shape
    f = w_gu.shape[2] // 2
    d = w_down.shape[2]
    n_valid = f // tn
    n_out = f_out // tn

    def w_block(j):
        return jnp.minimum(j, n_valid - 1)

    return pl.pallas_call(
        functools.partial(_ffn_up_kernel, n_valid=n_valid),
        out_shape=(jax.ShapeDtypeStruct((m, f_out), BF16),
                   jax.ShapeDtypeStruct((f_out + tn, d), BF16)),
        grid=(m // tm, n_out),
        in_specs=[pl.BlockSpec((tm, k), lambda i, j: (i, 0), pipeline_mode=pl.Buffered(1)),
                  pl.BlockSpec((None, k, tn), lambda i, j: (layer, 0, w_block(j))),
                  pl.BlockSpec((None, k, tn), lambda i, j: (layer, 0, n_valid + w_block(j))),
                  pl.BlockSpec((None, tn, d),
                               lambda i, j: (layer, jnp.where(i == 0, w_block(j), n_valid - 1), 0))],
        out_specs=(pl.BlockSpec((tm, tn), lambda i, j: (i, j)),
                   pl.BlockSpec((tn, d), lambda i, j: (jnp.where(i == 0, j, n_out), 0))),
        compiler_params=_params("arbitrary", "arbitrary"),
        name="ffn_up",
    )(x, w_gu, w_gu, w_down)


def _split3(x):
    a = x.astype(BF16)
    r = x - a.astype(F32)
    b = r.astype(BF16)
    c = (r - b.astype(F32)).astype(BF16)
    return a, b, c


def _hgrn_kernel(q_ref, f_ref, i_ref, gate_ref, lbl_ref, gn_ref, u_ref, o_ref,
                 q_s, k_s, b_s, p_s, st_s, d_s, *, layer, tc):
    n_chunks = tc // CHUNK
    n_sub = CHUNK // SUB

    @pl.when(pl.program_id(1) == 0)
    def _():
        st_s[...] = jnp.zeros_like(st_s)

    lbl = lbl_ref[...]
    e = jnp.exp(lbl - jnp.max(lbl, axis=0, keepdims=True))
    lb = jnp.sum(e[:layer + 1], axis=0, keepdims=True) / jnp.sum(e, axis=0, keepdims=True)

    row = lax.broadcasted_iota(jnp.int32, (CHUNK, CHUNK), 0)
    col = lax.broadcasted_iota(jnp.int32, (CHUNK, CHUNK), 1)
    tril = jnp.where(col <= row, 1.0, 0.0).astype(BF16)
    tril3 = jnp.concatenate([tril, tril, tril], axis=1)
    diag_mask = ((row // SUB) == (col // SUB)) & (col <= row)
    off_mask = (col // SUB) < (row // SUB)


    for c in range(n_chunks):
        rows = slice(c * CHUNK, (c + 1) * CHUNK)
        q = _silu(q_ref[rows, :])
        f = lb + (1.0 - lb) * _sigmoid(f_ref[rows, :])
        k = 1.0 - f
        g1, g2, g3 = _split3(jnp.log2(f))
        b = _dot(tril3, jnp.concatenate([g1, g2, g3], axis=0))
        q_s[rows, :] = q
        k_s[rows, :] = k
        b_s[rows, :] = b
        for i in range(0, n_sub, 2):
            r0 = c * CHUNK + i * SUB
            for s in range(SUB):
                halves = []
                for ii in (i, i + 1):
                    t = slice(ii * SUB, (ii + 1) * SUB)
                    r = c * CHUNK + ii * SUB + s
                    ks = jnp.broadcast_to(k_s[r:r + 1, :], (SUB, HEAD_DIM))
                    bs = jnp.broadcast_to(b_s[r:r + 1, :], (SUB, HEAD_DIM))
                    halves.append(q[t] * ks * jnp.exp2(jnp.minimum(b[t] - bs, 0.0)))
                p = jnp.concatenate(halves, axis=0)
                p_s[r0:r0 + 2 * SUB, s * HEAD_DIM:(s + 1) * HEAD_DIM] = p.astype(BF16)

    d_s[...] = _dot(p_s[...], u_ref[...])

    a_off, kv, st_decay = [], [], []
    for c in range(n_chunks):
        rows = slice(c * CHUNK, (c + 1) * CHUNK)
        q = q_s[rows, :]
        k = k_s[rows, :]
        b = b_s[rows, :]
        blocks = [jnp.zeros((SUB, CHUNK), F32)]
        for i in range(1, n_sub):
            n_keys = i * SUB
            bref = b[n_keys - 1:n_keys]
            qs = q[n_keys:n_keys + SUB] * jnp.exp2(b[n_keys:n_keys + SUB] - bref)
            ks = jnp.concatenate([k[:n_keys] * jnp.exp2(bref - b[:n_keys]), k[n_keys:]], axis=0)
            blocks.append(_dot_nt(qs.astype(BF16), ks.astype(BF16)))
        a_off.append(jnp.concatenate(blocks, axis=0))
        b_last = b[CHUNK - 1:CHUNK]
        vb = i_ref[rows, :].astype(BF16)
        kv.append(_dot_tn(vb, (k * jnp.exp2(b_last - b)).astype(BF16)))
        st_decay.append(jnp.exp2(b_last))

    o_intra = []
    for c in range(n_chunks):
        rows = slice(c * CHUNK, (c + 1) * CHUNK)
        d = d_s[rows, :][:, :CHUNK]
        a = jnp.where(diag_mask, d, jnp.where(off_mask, a_off[c], 0.0))
        o_intra.append(_dot(a.astype(BF16), i_ref[rows, :].astype(BF16)))

    gn = gn_ref[...]
    st = st_s[...]
    for c in range(n_chunks):
        rows = slice(c * CHUNK, (c + 1) * CHUNK)
        qb = (q_s[rows, :] * jnp.exp2(b_s[rows, :])).astype(BF16)
        o = _dot_nt(qb, st.astype(BF16)) + o_intra[c]
        st = st * st_decay[c] + kv[c]
        y = o * lax.rsqrt(jnp.mean(o * o, axis=-1, keepdims=True) + EPS) * gn
        o_ref[rows, :] = (y * _silu(gate_ref[rows, :])).astype(o_ref.dtype)
    st_s[...] = st


def hgrn_core(proj, lb_logits, gnorm, layer, tc=1024):
    s, d4 = proj.shape
    d = d4 // 4
    heads = d // HEAD_DIM
    n_l = lb_logits.shape[0]
    r = jnp.arange(SUB * HEAD_DIM)[:, None] // HEAD_DIM
    c = jnp.arange(HEAD_DIM)[None, :] % SUB
    u = (r == c).astype(BF16)

    def col(off):
        return pl.BlockSpec((tc, HEAD_DIM), lambda h, t: (t, off * heads + h))

    return pl.pallas_call(
        functools.partial(_hgrn_kernel, layer=layer, tc=tc),
        out_shape=jax.ShapeDtypeStruct((s, d), BF16),
        grid=(heads, s // tc),
        in_specs=[col(0), col(1), col(2), col(3),
                  pl.BlockSpec((n_l, HEAD_DIM), lambda h, t: (0, h)),
                  pl.BlockSpec((1, HEAD_DIM), lambda h, t: (0, 0)),
                  pl.BlockSpec((SUB * HEAD_DIM, HEAD_DIM), lambda h, t: (0, 0))],
        out_specs=pl.BlockSpec((tc, HEAD_DIM), lambda h, t: (t, h)),
        scratch_shapes=[pltpu.VMEM((tc, HEAD_DIM), F32),
                        pltpu.VMEM((tc, HEAD_DIM), F32),
                        pltpu.VMEM((tc, HEAD_DIM), F32),
                        pltpu.VMEM((tc, SUB * HEAD_DIM), BF16),
                        pltpu.VMEM((HEAD_DIM, HEAD_DIM), F32),
                        pltpu.VMEM((tc, HEAD_DIM), F32)],
        compiler_params=_params("parallel", "arbitrary"),
        name="hgrn_core",
    )(proj, proj, proj, proj, lb_logits, gnorm.reshape(1, HEAD_DIM), u)


def _sb_kernel(q_ref, k_ref, v_ref, w_ref, o_ref, acc_s, c_s, *, tq, sk, hpb):
    qi = pl.program_id(1)
    w = w_ref[...]
    acc_s[...] = jnp.zeros_like(acc_s)
    c_s[...] = jnp.zeros_like(c_s)
    n_sub = tq // sk

    def tile(k0, mask):
        for hh in range(hpb):
            hc = slice(hh * HEAD_DIM, (hh + 1) * HEAD_DIM)
            kb = k_ref[pl.ds(k0, tq), hc]
            vb = v_ref[pl.ds(k0, tq), hc]
            z = _dot_nt(q_ref[:, hc], kb)
            nz = -z
            lk = jnp.minimum(nz, 0.0) - jnp.log2(1.0 + jnp.exp2(jnp.minimum(z, nz)))
            if mask is not None:
                lk = jnp.where(mask, lk, 0.0)
            lkb = lk.astype(BF16)
            c = c_s[hh]
            parts = []
            for s in range(n_sub - 1, -1, -1):
                cols = slice(s * sk, (s + 1) * sk)
                r = _dot(lkb[:, cols], w)
                cc = jnp.concatenate([c] * (sk // HEAD_DIM), axis=1)
                parts.append(jnp.exp2(z[:, cols] + r + cc))
                c = c + jnp.broadcast_to(r[:, 0:1], c.shape)
            a = jnp.concatenate(parts[::-1], axis=1)
            if mask is not None:
                a = jnp.where(mask, a, 0.0)
            acc_s[hh] += _dot(a.astype(BF16), vb)
            c_s[hh] = c

    row = lax.broadcasted_iota(jnp.int32, (tq, tq), 0)
    col = lax.broadcasted_iota(jnp.int32, (tq, tq), 1)
    tile(pl.multiple_of(qi * tq, tq), col < row)

    def cond(carry):
        j, c_max = carry
        return jnp.logical_and(j < qi, c_max > SB_UNDERFLOW_LOG2)

    def body(carry):
        j, _ = carry
        tile(pl.multiple_of((qi - 1 - j) * tq, tq), None)
        return j + 1, jnp.max(c_s[...])

    lax.while_loop(cond, body, (jnp.int32(0), jnp.max(c_s[...])))
    for hh in range(hpb):
        o_ref[:, hh * HEAD_DIM:(hh + 1) * HEAD_DIM] = acc_s[hh].astype(o_ref.dtype)


def sb_core(qkv, tq=256, sk=256, hpb=4):
    s, d3 = qkv.shape
    d = d3 // 3
    groups = d // (hpb * HEAD_DIM)
    j = jnp.arange(sk)
    w = (j[:, None] >= j[None, :]).astype(BF16)
    kern = functools.partial(_sb_kernel, tq=tq, sk=sk, hpb=hpb)
    wide = hpb * HEAD_DIM
    return pl.pallas_call(
        kern,
        out_shape=jax.ShapeDtypeStruct((s, d), BF16),
        grid=(groups, s // tq),
        in_specs=[pl.BlockSpec((tq, wide), lambda h, i: (i, h)),
                  pl.BlockSpec((s, wide), lambda h, i: (0, groups + h)),
                  pl.BlockSpec((s, wide), lambda h, i: (0, 2 * groups + h)),
                  pl.BlockSpec((sk, sk), lambda h, i: (0, 0))],
        out_specs=pl.BlockSpec((tq, wide), lambda h, i: (i, h)),
        scratch_shapes=[pltpu.VMEM((hpb, tq, HEAD_DIM), F32),
                        pltpu.VMEM((hpb, tq, HEAD_DIM), F32)],
        compiler_params=_params("parallel", "arbitrary"),
        name="sb_core",
    )(qkv, qkv, qkv, w)


def _ffn_half_step(h, norm_w, w_gu, w_down, layer):
    f = w_down.shape[1]
    fp = -(-f // FF_PAD) * FF_PAD
    u = rmsnorm(h, norm_w[layer], BF16)
    a, down = ffn_up(u, w_gu, w_down, layer, fp, tn=FF_TILE)
    return matmul_residual(a, down, h, 0.5, tk=fp // 4)


def _cast_weight(w, layer):
    n = w.shape[2]
    tr = 8
    while 2 * tr * n <= 2 * 1024 * 1024 and w.shape[1] % (2 * tr) == 0:
        tr *= 2
    return cast_rows(w, layer, BF16, tr=tr)


def kernel(x, ffn1_norm, ffn1_w_gu, ffn1_w_down, mix_norm, ffn2_norm, ffn2_w_gu, ffn2_w_down,
           hgrn_w_in, hgrn_lb_logits, hgrn_gnorm, hgrn_w_out, sb_w_in, sb_w_out, final_norm):
    b, s, d = x.shape
    depth = ffn1_norm.shape[0]
    outs = []
    for bi in range(b):
        h = x[bi]
        for i in range(depth):
            h = _ffn_half_step(h, ffn1_norm, ffn1_w_gu, ffn1_w_down, i)
            u = rmsnorm(h, mix_norm[i], BF16)
            j = i // 2
            if i % 2 == 0:
                proj = matmul(u, hgrn_w_in, j, F32)
                o = hgrn_core(proj, hgrn_lb_logits, hgrn_gnorm[j], layer=i)
                h = matmul_residual(o, _cast_weight(hgrn_w_out, j), h, 1.0)
            else:
                qkv = matmul(u, sb_w_in, j, BF16, scaled_cols=d,
                             scale=math.log2(math.e) / math.sqrt(HEAD_DIM))
                o = sb_core(qkv)
                h = matmul_residual(o, _cast_weight(sb_w_out, j), h, 1.0)
            h = _ffn_half_step(h, ffn2_norm, ffn2_w_gu, ffn2_w_down, i)
        outs.append(rmsnorm(h, final_norm, x.dtype))
    return jnp.stack(outs, axis=0)
```

```python
import functools
import math

import jax
import jax.numpy as jnp
from jax import lax
from jax.experimental import pallas as pl
from jax.experimental.pallas import tpu as pltpu

EPS = 1e-6
CHUNK = 64
SUB = 8
HEAD_DIM = 128
FF_PAD = 1024
FF_TILE = 256
FFN_UP_ROWS = 1024
MATMUL_ROWS = 1024
F32_MIN_SUBNORMAL_LOG2 = -149.0
SB_UNDERFLOW_LOG2 = F32_MIN_SUBNORMAL_LOG2 - 43.0

VMEM_LIMIT_BYTES = 56 * 1024 * 1024

BF16 = jnp.bfloat16
F32 = jnp.float32


def _params(*sem):
    return pltpu.CompilerParams(dimension_semantics=sem, vmem_limit_bytes=VMEM_LIMIT_BYTES)


def _dot(a, b):
    return jnp.dot(a, b, preferred_element_type=F32)


def _dot_nt(a, b):
    return lax.dot_general(a, b, (((1,), (1,)), ((), ())), preferred_element_type=F32)


def _dot_tn(a, b):
    return lax.dot_general(a, b, (((0,), (0,)), ((), ())), preferred_element_type=F32)


def _sigmoid(x):
    return 1.0 / (1.0 + jnp.exp(-x))


def _silu(x):
    return x * _sigmoid(x)


def _rmsnorm_kernel(x_ref, w_ref, o_ref):
    x = x_ref[...]
    y = x * lax.rsqrt(jnp.mean(x * x, axis=-1, keepdims=True) + EPS)
    o_ref[...] = (y * w_ref[...]).astype(o_ref.dtype)


def rmsnorm(x, w, out_dtype, tm=512):
    s, d = x.shape
    return pl.pallas_call(
        _rmsnorm_kernel,
        out_shape=jax.ShapeDtypeStruct((s, d), out_dtype),
        grid=(s // tm,),
        in_specs=[pl.BlockSpec((tm, d), lambda i: (i, 0)),
                  pl.BlockSpec((1, d), lambda i: (0, 0))],
        out_specs=pl.BlockSpec((tm, d), lambda i: (i, 0)),
        compiler_params=_params("parallel"),
        name="rmsnorm",
    )(x, w.reshape(1, d))


def _cast_kernel(x_ref, o_ref):
    o_ref[...] = x_ref[...].astype(o_ref.dtype)


def cast_rows(w, layer, dtype, tr):
    _, r, n = w.shape
    return pl.pallas_call(
        _cast_kernel,
        out_shape=jax.ShapeDtypeStruct((r, n), dtype),
        grid=(r // tr,),
        in_specs=[pl.BlockSpec((None, tr, n), lambda i: (layer, i, 0))],
        out_specs=pl.BlockSpec((tr, n), lambda i: (i, 0)),
        compiler_params=_params("parallel"),
        name="cast_rows",
    )(w)


def _matmul_kernel(x_ref, w_ref, o_ref, *, scaled_blocks, scale):
    w = w_ref[...].astype(BF16)
    for r in range(0, x_ref.shape[0], MATMUL_ROWS):
        rows = slice(r, r + MATMUL_ROWS)
        acc = _dot(x_ref[rows, :], w)
        if scaled_blocks:
            acc = acc * jnp.where(pl.program_id(1) < scaled_blocks, scale, 1.0)
        o_ref[rows, :] = acc.astype(o_ref.dtype)


def matmul(x, w, layer, out_dtype, tm=2048, tn=512, scaled_cols=0, scale=1.0):
    m, k = x.shape
    n = w.shape[2]
    return pl.pallas_call(
        functools.partial(_matmul_kernel, scaled_blocks=scaled_cols // tn, scale=scale),
        out_shape=jax.ShapeDtypeStruct((m, n), out_dtype),
        grid=(m // tm, n // tn),
        in_specs=[pl.BlockSpec((tm, k), lambda i, j: (i, 0), pipeline_mode=pl.Buffered(1)),
                  pl.BlockSpec((None, k, tn), lambda i, j: (layer, 0, j))],
        out_specs=pl.BlockSpec((tm, tn), lambda i, j: (i, j)),
        compiler_params=_params("parallel", "arbitrary"),
        name="matmul",
    )(x, w)


def _matmul_residual_kernel(x_ref, w_ref, h_ref, o_ref, *acc, scale, nk):
    if nk == 1:
        o_ref[...] = h_ref[...] + scale * _dot(x_ref[...], w_ref[...])
        return
    acc_s, = acc
    kk = pl.program_id(2)

    @pl.when(kk == 0)
    def _():
        acc_s[...] = _dot(x_ref[...], w_ref[...])

    @pl.when(jnp.logical_and(kk > 0, kk < nk - 1))
    def _():
        acc_s[...] += _dot(x_ref[...], w_ref[...])

    @pl.when(kk == nk - 1)
    def _():
        o_ref[...] = h_ref[...] + scale * (acc_s[...] + _dot(x_ref[...], w_ref[...]))


def matmul_residual(x, w, h, scale, tm=1024, tn=1024, tk=None):
    m, k = x.shape
    n = w.shape[1]
    tk = k if tk is None else tk
    nk = k // tk
    return pl.pallas_call(
        functools.partial(_matmul_residual_kernel, scale=scale, nk=nk),
        out_shape=jax.ShapeDtypeStruct((m, n), F32),
        grid=(m // tm, n // tn, nk),
        in_specs=[pl.BlockSpec((tm, tk), lambda i, j, kk: (i, kk)),
                  pl.BlockSpec((tk, tn), lambda i, j, kk: (kk, j)),
                  pl.BlockSpec((tm, tn), lambda i, j, kk: (i, j))],
        out_specs=pl.BlockSpec((tm, tn), lambda i, j, kk: (i, j)),
        scratch_shapes=[pltpu.VMEM((tm, tn), F32)] if nk > 1 else [],
        compiler_params=_params("parallel", "parallel", "arbitrary"),
        name="matmul_residual",
    )(x, w, h)


def _ffn_up_kernel(x_ref, wg_ref, wu_ref, wd_ref, o_ref, d_ref, *, n_valid):
    j = pl.program_id(1)

    @pl.when(j < n_valid)
    def _():
        wg = wg_ref[...].astype(BF16)
        wu = wu_ref[...].astype(BF16)
        for r in range(0, x_ref.shape[0], FFN_UP_ROWS):
            rows = slice(r, r + FFN_UP_ROWS)
            x = x_ref[rows, :]
            o_ref[rows, :] = (_silu(_dot(x, wg)) * _dot(x, wu)).astype(o_ref.dtype)
        d_ref[...] = wd_ref[...].astype(d_ref.dtype)

    @pl.when(j >= n_valid)
    def _():
        o_ref[...] = jnp.zeros_like(o_ref)
        d_ref[...] = jnp.zeros_like(d_ref)


def ffn_up(x, w_gu, w_down, layer, f_out, tm=2048, tn=256):
    m, k = x.shape
    f = w_gu.shape[2] // 2
    d = w_down.shape[2]
    n_valid = f // tn
    n_out = f_out // tn

    def w_block(j):
        return jnp.minimum(j, n_valid - 1)

    return pl.pallas_call(
        functools.partial(_ffn_up_kernel, n_valid=n_valid),
        out_shape=(jax.ShapeDtypeStruct((m, f_out), BF16),
                   jax.ShapeDtypeStruct((f_out + tn, d), BF16)),
        grid=(m // tm, n_out),
        in_specs=[pl.BlockSpec((tm, k), lambda i, j: (i, 0), pipeline_mode=pl.Buffered(1)),
                  pl.BlockSpec((None, k, tn), lambda i, j: (layer, 0, w_block(j))),
                  pl.BlockSpec((None, k, tn), lambda i, j: (layer, 0, n_valid + w_block(j))),
                  pl.BlockSpec((None, tn, d),
                               lambda i, j: (layer, jnp.where(i == 0, w_block(j), n_valid - 1), 0))],
        out_specs=(pl.BlockSpec((tm, tn), lambda i, j: (i, j)),
                   pl.BlockSpec((tn, d), lambda i, j: (jnp.where(i == 0, j, n_out), 0))),
        compiler_params=_params("arbitrary", "arbitrary"),
        name="ffn_up",
    )(x, w_gu, w_gu, w_down)


def _split3(x):
    a = x.astype(BF16)
    r = x - a.astype(F32)
    b = r.astype(BF16)
    c = (r - b.astype(F32)).astype(BF16)
    return a, b, c


def _hgrn_kernel(q_ref, f_ref, i_ref, gate_ref, lbl_ref, gn_ref, u_ref, o_ref,
                 q_s, k_s, b_s, p_s, st_s, d_s, *, layer, tc):
    n_chunks = tc // CHUNK
    n_sub = CHUNK // SUB

    @pl.when(pl.program_id(1) == 0)
    def _():
        st_s[...] = jnp.zeros_like(st_s)

    lbl = lbl_ref[...]
    e = jnp.exp(lbl - jnp.max(lbl, axis=0, keepdims=True))
    lb = jnp.sum(e[:layer + 1], axis=0, keepdims=True) / jnp.sum(e, axis=0, keepdims=True)

    row = lax.broadcasted_iota(jnp.int32, (CHUNK, CHUNK), 0)
    col = lax.broadcasted_iota(jnp.int32, (CHUNK, CHUNK), 1)
    tril = jnp.where(col <= row, 1.0, 0.0).astype(BF16)
    tril3 = jnp.concatenate([tril, tril, tril], axis=1)
    diag_mask = ((row // SUB) == (col // SUB)) & (col <= row)
    off_mask = (col // SUB) < (row // SUB)


    for c in range(n_chunks):
        rows = slice(c * CHUNK, (c + 1) * CHUNK)
        q = _silu(q_ref[rows, :])
        f = lb + (1.0 - lb) * _sigmoid(f_ref[rows, :])
        k = 1.0 - f
        g1, g2, g3 = _split3(jnp.log2(f))
        b = _dot(tril3, jnp.concatenate([g1, g2, g3], axis=0))
        q_s[rows, :] = q
        k_s[rows, :] = k
        b_s[rows, :] = b
        for i in range(0, n_sub, 2):
            r0 = c * CHUNK + i * SUB
            for s in range(SUB):
                halves = []
                for ii in (i, i + 1):
                    t = slice(ii * SUB, (ii + 1) * SUB)
                    r = c * CHUNK + ii * SUB + s
                    ks = jnp.broadcast_to(k_s[r:r + 1, :], (SUB, HEAD_DIM))
                    bs = jnp.broadcast_to(b_s[r:r + 1, :], (SUB, HEAD_DIM))
                    halves.append(q[t] * ks * jnp.exp2(jnp.minimum(b[t] - bs, 0.0)))
                p = jnp.concatenate(halves, axis=0)
                p_s[r0:r0 + 2 * SUB, s * HEAD_DIM:(s + 1) * HEAD_DIM] = p.astype(BF16)

    d_s[...] = _dot(p_s[...], u_ref[...])

    a_off, kv, st_decay = [], [], []
    for c in range(n_chunks):
        rows = slice(c * CHUNK, (c + 1) * CHUNK)
        q = q_s[rows, :]
        k = k_s[rows, :]
        b = b_s[rows, :]
        blocks = [jnp.zeros((SUB, CHUNK), F32)]
        for i in range(1, n_sub):
            n_keys = i * SUB
            bref = b[n_keys - 1:n_keys]
            qs = q[n_keys:n_keys + SUB] * jnp.exp2(b[n_keys:n_keys + SUB] - bref)
            ks = jnp.concatenate([k[:n_keys] * jnp.exp2(bref - b[:n_keys]), k[n_keys:]], axis=0)
            blocks.append(_dot_nt(qs.astype(BF16), ks.astype(BF16)))
        a_off.append(jnp.concatenate(blocks, axis=0))
        b_last = b[CHUNK - 1:CHUNK]
        vb = i_ref[rows, :].astype(BF16)
        kv.append(_dot_tn(vb, (k * jnp.exp2(b_last - b)).astype(BF16)))
        st_decay.append(jnp.exp2(b_last))

    o_intra = []
    for c in range(n_chunks):
        rows = slice(c * CHUNK, (c + 1) * CHUNK)
        d = d_s[rows, :][:, :CHUNK]
        a = jnp.where(diag_mask, d, jnp.where(off_mask, a_off[c], 0.0))
        o_intra.append(_dot(a.astype(BF16), i_ref[rows, :].astype(BF16)))

    gn = gn_ref[...]
    st = st_s[...]
    for c in range(n_chunks):
        rows = slice(c * CHUNK, (c + 1) * CHUNK)
        qb = (q_s[rows, :] * jnp.exp2(b_s[rows, :])).astype(BF16)
        o = _dot_nt(qb, st.astype(BF16)) + o_intra[c]
        st = st * st_decay[c] + kv[c]
        y = o * lax.rsqrt(jnp.mean(o * o, axis=-1, keepdims=True) + EPS) * gn
        o_ref[rows, :] = (y * _silu(gate_ref[rows, :])).astype(o_ref.dtype)
    st_s[...] = st


def hgrn_core(proj, lb_logits, gnorm, layer, tc=1024):
    s, d4 = proj.shape
    d = d4 // 4
    heads = d // HEAD_DIM
    n_l = lb_logits.shape[0]
    r = jnp.arange(SUB * HEAD_DIM)[:, None] // HEAD_DIM
    c = jnp.arange(HEAD_DIM)[None, :] % SUB
    u = (r == c).astype(BF16)

    def col(off):
        return pl.BlockSpec((tc, HEAD_DIM), lambda h, t: (t, off * heads + h))

    return pl.pallas_call(
        functools.partial(_hgrn_kernel, layer=layer, tc=tc),
        out_shape=jax.ShapeDtypeStruct((s, d), BF16),
        grid=(heads, s // tc),
        in_specs=[col(0), col(1), col(2), col(3),
                  pl.BlockSpec((n_l, HEAD_DIM), lambda h, t: (0, h)),
                  pl.BlockSpec((1, HEAD_DIM), lambda h, t: (0, 0)),
                  pl.BlockSpec((SUB * HEAD_DIM, HEAD_DIM), lambda h, t: (0, 0))],
        out_specs=pl.BlockSpec((tc, HEAD_DIM), lambda h, t: (t, h)),
        scratch_shapes=[pltpu.VMEM((tc, HEAD_DIM), F32),
                        pltpu.VMEM((tc, HEAD_DIM), F32),
                        pltpu.VMEM((tc, HEAD_DIM), F32),
                        pltpu.VMEM((tc, SUB * HEAD_DIM), BF16),
                        pltpu.VMEM((HEAD_DIM, HEAD_DIM), F32),
                        pltpu.VMEM((tc, HEAD_DIM), F32)],
        compiler_params=_params("parallel", "arbitrary"),
        name="hgrn_core",
    )(proj, proj, proj, proj, lb_logits, gnorm.reshape(1, HEAD_DIM), u)


def _sb_kernel(q_ref, k_ref, v_ref, w_ref, o_ref, acc_s, c_s, *, tq, hpb):
    qi = pl.program_id(1)
    w = w_ref[...]
    acc_s[...] = jnp.zeros_like(acc_s)
    c_s[...] = jnp.zeros_like(c_s)

    def walk(k0, masks):
        n_sub = len(masks)
        for hh in range(hpb):
            hc = slice(hh * HEAD_DIM, (hh + 1) * HEAD_DIM)
            kb = k_ref[pl.ds(k0, n_sub * tq), hc]
            vb = v_ref[pl.ds(k0, n_sub * tq), hc]
            z = _dot_nt(q_ref[:, hc], kb)
            c = c_s[hh]
            parts = []
            for s in range(n_sub - 1, -1, -1):
                zs = z[:, s * tq:(s + 1) * tq]
                nz = -zs
                lk = jnp.minimum(nz, 0.0) - jnp.log2(1.0 + jnp.exp2(jnp.minimum(zs, nz)))
                if masks[s] is not None:
                    lk = jnp.where(masks[s], lk, 0.0)
                r = _dot(lk.astype(BF16), w)
                cc = jnp.concatenate([c] * (tq // HEAD_DIM), axis=1)
                a = jnp.exp2(zs + r + cc)
                if masks[s] is not None:
                    a = jnp.where(masks[s], a, 0.0)
                parts.append(a)
                c = c + jnp.broadcast_to(r[:, 0:1], c.shape)
            a = parts[0] if n_sub == 1 else jnp.concatenate(parts[::-1], axis=1)
            acc_s[hh] += _dot(a.astype(BF16), vb)
            c_s[hh] = c

    row = lax.broadcasted_iota(jnp.int32, (tq, tq), 0)
    col = lax.broadcasted_iota(jnp.int32, (tq, tq), 1)
    causal = col < row

    @pl.when(qi == 0)
    def _():
        walk(0, [causal])

    @pl.when(qi > 0)
    def _():
        walk(pl.multiple_of((qi - 1) * tq, tq), [None, causal])

    n_left = jnp.maximum(qi - 1, 0)

    def cond(carry):
        j, c_max = carry
        return jnp.logical_and(j < n_left, c_max > SB_UNDERFLOW_LOG2)

    def body(carry):
        j, _ = carry
        walk(pl.multiple_of((n_left - 1 - j) * tq, tq), [None])
        return j + 1, jnp.max(c_s[...])

    lax.while_loop(cond, body, (jnp.int32(0), jnp.max(c_s[...])))
    for hh in range(hpb):
        o_ref[:, hh * HEAD_DIM:(hh + 1) * HEAD_DIM] = acc_s[hh].astype(o_ref.dtype)


def sb_core(qkv, tq=256, hpb=4):
    s, d3 = qkv.shape
    d = d3 // 3
    groups = d // (hpb * HEAD_DIM)
    j = jnp.arange(tq)
    w = (j[:, None] >= j[None, :]).astype(BF16)
    kern = functools.partial(_sb_kernel, tq=tq, hpb=hpb)
    wide = hpb * HEAD_DIM
    return pl.pallas_call(
        kern,
        out_shape=jax.ShapeDtypeStruct((s, d), BF16),
        grid=(groups, s // tq),
        in_specs=[pl.BlockSpec((tq, wide), lambda h, i: (i, h)),
                  pl.BlockSpec((s, wide), lambda h, i: (0, groups + h)),
                  pl.BlockSpec((s, wide), lambda h, i: (0, 2 * groups + h)),
                  pl.BlockSpec((tq, tq), lambda h, i: (0, 0))],
        out_specs=pl.BlockSpec((tq, wide), lambda h, i: (i, h)),
        scratch_shapes=[pltpu.VMEM((hpb, tq, HEAD_DIM), F32),
                        pltpu.VMEM((hpb, tq, HEAD_DIM), F32)],
        compiler_params=_params("parallel", "arbitrary"),
        name="sb_core",
    )(qkv, qkv, qkv, w)


def _ffn_half_step(h, norm_w, w_gu, w_down, layer):
    f = w_down.shape[1]
    fp = -(-f // FF_PAD) * FF_PAD
    u = rmsnorm(h, norm_w[layer], BF16)
    a, down = ffn_up(u, w_gu, w_down, layer, fp, tn=FF_TILE)
    return matmul_residual(a, down, h, 0.5, tk=fp // 4)


def _cast_weight(w, layer):
    n = w.shape[2]
    tr = 8
    while 2 * tr * n <= 2 * 1024 * 1024 and w.shape[1] % (2 * tr) == 0:
        tr *= 2
    return cast_rows(w, layer, BF16, tr=tr)


def kernel(x, ffn1_norm, ffn1_w_gu, ffn1_w_down, mix_norm, ffn2_norm, ffn2_w_gu, ffn2_w_down,
           hgrn_w_in, hgrn_lb_logits, hgrn_gnorm, hgrn_w_out, sb_w_in, sb_w_out, final_norm):
    b, s, d = x.shape
    depth = ffn1_norm.shape[0]
    outs = []
    for bi in range(b):
        h = x[bi]
        for i in range(depth):
            h = _ffn_half_step(h, ffn1_norm, ffn1_w_gu, ffn1_w_down, i)
            u = rmsnorm(h, mix_norm[i], BF16)
            j = i // 2
            if i % 2 == 0:
                proj = matmul(u, hgrn_w_in, j, F32)
                o = hgrn_core(proj, hgrn_lb_logits, hgrn_gnorm[j], layer=i)
                h = matmul_residual(o, _cast_weight(hgrn_w_out, j), h, 1.0)
            else:
                qkv = matmul(u, sb_w_in, j, BF16, scaled_cols=d,
                             scale=math.log2(math.e) / math.sqrt(HEAD_DIM))
                o = sb_core(qkv)
                h = matmul_residual(o, _cast_weight(sb_w_out, j), h, 1.0)
            h = _ffn_half_step(h, ffn2_norm, ffn2_w_gu, ffn2_w_down, i)
        outs.append(rmsnorm(h, final_norm, x.dtype))
    return jnp.stack(outs, axis=0)
```

```python
import functools
import math

import jax
import jax.numpy as jnp
from jax import lax
from jax.experimental import pallas as pl
from jax.experimental.pallas import tpu as pltpu

EPS = 1e-6
CHUNK = 64
SUB = 8
HEAD_DIM = 128
FF_PAD = 1024
FF_TILE = 256
FFN_UP_ROWS = 512
MATMUL_ROWS = 1024
F32_MIN_SUBNORMAL_LOG2 = -149.0
SB_UNDERFLOW_LOG2 = F32_MIN_SUBNORMAL_LOG2 - 43.0

VMEM_LIMIT_BYTES = 56 * 1024 * 1024

BF16 = jnp.bfloat16
F32 = jnp.float32


def _params(*sem):
    return pltpu.CompilerParams(dimension_semantics=sem, vmem_limit_bytes=VMEM_LIMIT_BYTES)


def _dot(a, b):
    return jnp.dot(a, b, preferred_element_type=F32)


def _dot_nt(a, b):
    return lax.dot_general(a, b, (((1,), (1,)), ((), ())), preferred_element_type=F32)


def _dot_tn(a, b):
    return lax.dot_general(a, b, (((0,), (0,)), ((), ())), preferred_element_type=F32)


def _sigmoid(x):
    return 1.0 / (1.0 + jnp.exp(-x))


def _silu(x):
    return x * _sigmoid(x)


def _rmsnorm_kernel(x_ref, w_ref, o_ref):
    x = x_ref[...]
    y = x * lax.rsqrt(jnp.mean(x * x, axis=-1, keepdims=True) + EPS)
    o_ref[...] = (y * w_ref[...]).astype(o_ref.dtype)


def rmsnorm(x, w, out_dtype, tm=512):
    s, d = x.shape
    return pl.pallas_call(
        _rmsnorm_kernel,
        out_shape=jax.ShapeDtypeStruct((s, d), out_dtype),
        grid=(s // tm,),
        in_specs=[pl.BlockSpec((tm, d), lambda i: (i, 0)),
                  pl.BlockSpec((1, d), lambda i: (0, 0))],
        out_specs=pl.BlockSpec((tm, d), lambda i: (i, 0)),
        compiler_params=_params("parallel"),
        name="rmsnorm",
    )(x, w.reshape(1, d))


def _cast_kernel(x_ref, o_ref):
    o_ref[...] = x_ref[...].astype(o_ref.dtype)


def cast_rows(w, layer, dtype, tr):
    _, r, n = w.shape
    return pl.pallas_call(
        _cast_kernel,
        out_shape=jax.ShapeDtypeStruct((r, n), dtype),
        grid=(r // tr,),
        in_specs=[pl.BlockSpec((None, tr, n), lambda i: (layer, i, 0))],
        out_specs=pl.BlockSpec((tr, n), lambda i: (i, 0)),
        compiler_params=_params("parallel"),
        name="cast_rows",
    )(w)


def _matmul_kernel(x_ref, w_ref, o_ref, *, scaled_blocks, scale):
    w = w_ref[...].astype(BF16)
    for r in range(0, x_ref.shape[0], MATMUL_ROWS):
        rows = slice(r, r + MATMUL_ROWS)
        acc = _dot(x_ref[rows, :], w)
        if scaled_blocks:
            acc = acc * jnp.where(pl.program_id(1) < scaled_blocks, scale, 1.0)
        o_ref[rows, :] = acc.astype(o_ref.dtype)


def matmul(x, w, layer, out_dtype, tm=2048, tn=512, scaled_cols=0, scale=1.0):
    m, k = x.shape
    n = w.shape[2]
    return pl.pallas_call(
        functools.partial(_matmul_kernel, scaled_blocks=scaled_cols // tn, scale=scale),
        out_shape=jax.ShapeDtypeStruct((m, n), out_dtype),
        grid=(m // tm, n // tn),
        in_specs=[pl.BlockSpec((tm, k), lambda i, j: (i, 0), pipeline_mode=pl.Buffered(1)),
                  pl.BlockSpec((None, k, tn), lambda i, j: (layer, 0, j))],
        out_specs=pl.BlockSpec((tm, tn), lambda i, j: (i, j)),
        compiler_params=_params("parallel", "arbitrary"),
        name="matmul",
    )(x, w)


def _matmul_residual_kernel(x_ref, w_ref, h_ref, o_ref, *acc, scale, nk):
    if nk == 1:
        o_ref[...] = h_ref[...] + scale * _dot(x_ref[...], w_ref[...])
        return
    acc_s, = acc
    kk = pl.program_id(2)

    @pl.when(kk == 0)
    def _():
        acc_s[...] = _dot(x_ref[...], w_ref[...])

    @pl.when(jnp.logical_and(kk > 0, kk < nk - 1))
    def _():
        acc_s[...] += _dot(x_ref[...], w_ref[...])

    @pl.when(kk == nk - 1)
    def _():
        o_ref[...] = h_ref[...] + scale * (acc_s[...] + _dot(x_ref[...], w_ref[...]))


def matmul_residual(x, w, h, scale, tm=1024, tn=1024, tk=None):
    m, k = x.shape
    n = w.shape[1]
    tk = k if tk is None else tk
    nk = k // tk
    return pl.pallas_call(
        functools.partial(_matmul_residual_kernel, scale=scale, nk=nk),
        out_shape=jax.ShapeDtypeStruct((m, n), F32),
        grid=(m // tm, n // tn, nk),
        in_specs=[pl.BlockSpec((tm, tk), lambda i, j, kk: (i, kk)),
                  pl.BlockSpec((tk, tn), lambda i, j, kk: (kk, j)),
                  pl.BlockSpec((tm, tn), lambda i, j, kk: (i, j))],
        out_specs=pl.BlockSpec((tm, tn), lambda i, j, kk: (i, j)),
        scratch_shapes=[pltpu.VMEM((tm, tn), F32)] if nk > 1 else [],
        compiler_params=_params("parallel", "parallel", "arbitrary"),
        name="matmul_residual",
    )(x, w, h)


def _ffn_up_kernel(x_ref, wg_ref, wu_ref, wd_ref, o_ref, d_ref, *, n_valid):
    j = pl.program_id(1)

    @pl.when(j < n_valid)
    def _():
        wg = wg_ref[...].astype(BF16)
        wu = wu_ref[...].astype(BF16)
        n_rows = x_ref.shape[0] // FFN_UP_ROWS
        d_rows = d_ref.shape[0] // n_rows
        for r in range(n_rows):
            rows = slice(r * FFN_UP_ROWS, (r + 1) * FFN_UP_ROWS)
            x = x_ref[rows, :]
            o_ref[rows, :] = (_silu(_dot(x, wg)) * _dot(x, wu)).astype(o_ref.dtype)
            dr = slice(r * d_rows, (r + 1) * d_rows)
            d_ref[dr, :] = wd_ref[dr, :].astype(d_ref.dtype)

    @pl.when(j >= n_valid)
    def _():
        o_ref[...] = jnp.zeros_like(o_ref)
        d_ref[...] = jnp.zeros_like(d_ref)


def ffn_up(x, w_gu, w_down, layer, f_out, tm=2048, tn=256):
    m, k = x.shape
    f = w_gu.shape[2] // 2
    d = w_down.shape[2]
    n_valid = f // tn
    n_out = f_out // tn

    def w_block(j):
        return jnp.minimum(j, n_valid - 1)

    return pl.pallas_call(
        functools.partial(_ffn_up_kernel, n_valid=n_valid),
        out_shape=(jax.ShapeDtypeStruct((m, f_out), BF16),
                   jax.ShapeDtypeStruct((f_out + tn, d), BF16)),
        grid=(m // tm, n_out),
        in_specs=[pl.BlockSpec((tm, k), lambda i, j: (i, 0), pipeline_mode=pl.Buffered(1)),
                  pl.BlockSpec((None, k, tn), lambda i, j: (layer, 0, w_block(j))),
                  pl.BlockSpec((None, k, tn), lambda i, j: (layer, 0, n_valid + w_block(j))),
                  pl.BlockSpec((None, tn, d),
                               lambda i, j: (layer, jnp.where(i == 0, w_block(j), n_valid - 1), 0))],
        out_specs=(pl.BlockSpec((tm, tn), lambda i, j: (i, j)),
                   pl.BlockSpec((tn, d), lambda i, j: (jnp.where(i == 0, j, n_out), 0))),
        compiler_params=_params("arbitrary", "arbitrary"),
        name="ffn_up",
    )(x, w_gu, w_gu, w_down)


def _split3(x):
    a = x.astype(BF16)
    r = x - a.astype(F32)
    b = r.astype(BF16)
    c = (r - b.astype(F32)).astype(BF16)
    return a, b, c


def _hgrn_kernel(q_ref, f_ref, i_ref, gate_ref, lbl_ref, gn_ref, u_ref, o_ref,
                 q_s, k_s, b_s, p_s, st_s, d_s, *, layer, tc):
    n_chunks = tc // CHUNK
    n_sub = CHUNK // SUB

    @pl.when(pl.program_id(1) == 0)
    def _():
        st_s[...] = jnp.zeros_like(st_s)

    lbl = lbl_ref[...]
    e = jnp.exp(lbl - jnp.max(lbl, axis=0, keepdims=True))
    lb = jnp.sum(e[:layer + 1], axis=0, keepdims=True) / jnp.sum(e, axis=0, keepdims=True)

    row = lax.broadcasted_iota(jnp.int32, (CHUNK, CHUNK), 0)
    col = lax.broadcasted_iota(jnp.int32, (CHUNK, CHUNK), 1)
    tril = jnp.where(col <= row, 1.0, 0.0).astype(BF16)
    tril3 = jnp.concatenate([tril, tril, tril], axis=1)
    diag_mask = ((row // SUB) == (col // SUB)) & (col <= row)
    off_mask = (col // SUB) < (row // SUB)


    for c in range(n_chunks):
        rows = slice(c * CHUNK, (c + 1) * CHUNK)
        q = _silu(q_ref[rows, :])
        f = lb + (1.0 - lb) * _sigmoid(f_ref[rows, :])
        k = 1.0 - f
        g1, g2, g3 = _split3(jnp.log2(f))
        b = _dot(tril3, jnp.concatenate([g1, g2, g3], axis=0))
        q_s[rows, :] = q
        k_s[rows, :] = k
        b_s[rows, :] = b
        for i in range(0, n_sub, 2):
            r0 = c * CHUNK + i * SUB
            for s in range(SUB):
                halves = []
                for ii in (i, i + 1):
                    t = slice(ii * SUB, (ii + 1) * SUB)
                    r = c * CHUNK + ii * SUB + s
                    ks = jnp.broadcast_to(k_s[r:r + 1, :], (SUB, HEAD_DIM))
                    bs = jnp.broadcast_to(b_s[r:r + 1, :], (SUB, HEAD_DIM))
                    halves.append(q[t] * ks * jnp.exp2(jnp.minimum(b[t] - bs, 0.0)))
                p = jnp.concatenate(halves, axis=0)
                p_s[r0:r0 + 2 * SUB, s * HEAD_DIM:(s + 1) * HEAD_DIM] = p.astype(BF16)

    d_s[...] = _dot(p_s[...], u_ref[...])

    a_off, kv, st_decay = [], [], []
    for c in range(n_chunks):
        rows = slice(c * CHUNK, (c + 1) * CHUNK)
        q = q_s[rows, :]
        k = k_s[rows, :]
        b = b_s[rows, :]
        blocks = [jnp.zeros((SUB, CHUNK), F32)]
        for i in range(1, n_sub):
            n_keys = i * SUB
            bref = b[n_keys - 1:n_keys]
            qs = q[n_keys:n_keys + SUB] * jnp.exp2(b[n_keys:n_keys + SUB] - bref)
            ks = jnp.concatenate([k[:n_keys] * jnp.exp2(bref - b[:n_keys]), k[n_keys:]], axis=0)
            blocks.append(_dot_nt(qs.astype(BF16), ks.astype(BF16)))
        a_off.append(jnp.concatenate(blocks, axis=0))
        b_last = b[CHUNK - 1:CHUNK]
        vb = i_ref[rows, :].astype(BF16)
        kv.append(_dot_tn(vb, (k * jnp.exp2(b_last - b)).astype(BF16)))
        st_decay.append(jnp.exp2(b_last))

    o_intra = []
    for c in range(n_chunks):
        rows = slice(c * CHUNK, (c + 1) * CHUNK)
        d = d_s[rows, :][:, :CHUNK]
        a = jnp.where(diag_mask, d, jnp.where(off_mask, a_off[c], 0.0))
        o_intra.append(_dot(a.astype(BF16), i_ref[rows, :].astype(BF16)))

    gn = gn_ref[...]
    st = st_s[...]
    for c in range(n_chunks):
        rows = slice(c * CHUNK, (c + 1) * CHUNK)
        qb = (q_s[rows, :] * jnp.exp2(b_s[rows, :])).astype(BF16)
        o = _dot_nt(qb, st.astype(BF16)) + o_intra[c]
        st = st * st_decay[c] + kv[c]
        y = o * lax.rsqrt(jnp.mean(o * o, axis=-1, keepdims=True) + EPS) * gn
        o_ref[rows, :] = (y * _silu(gate_ref[rows, :])).astype(o_ref.dtype)
    st_s[...] = st


def hgrn_core(proj, lb_logits, gnorm, layer, tc=1024):
    s, d4 = proj.shape
    d = d4 // 4
    heads = d // HEAD_DIM
    n_l = lb_logits.shape[0]
    r = jnp.arange(SUB * HEAD_DIM)[:, None] // HEAD_DIM
    c = jnp.arange(HEAD_DIM)[None, :] % SUB
    u = (r == c).astype(BF16)

    def col(off):
        return pl.BlockSpec((tc, HEAD_DIM), lambda h, t: (t, off * heads + h))

    return pl.pallas_call(
        functools.partial(_hgrn_kernel, layer=layer, tc=tc),
        out_shape=jax.ShapeDtypeStruct((s, d), BF16),
        grid=(heads, s // tc),
        in_specs=[col(0), col(1), col(2), col(3),
                  pl.BlockSpec((n_l, HEAD_DIM), lambda h, t: (0, h)),
                  pl.BlockSpec((1, HEAD_DIM), lambda h, t: (0, 0)),
                  pl.BlockSpec((SUB * HEAD_DIM, HEAD_DIM), lambda h, t: (0, 0))],
        out_specs=pl.BlockSpec((tc, HEAD_DIM), lambda h, t: (t, h)),
        scratch_shapes=[pltpu.VMEM((tc, HEAD_DIM), F32),
                        pltpu.VMEM((tc, HEAD_DIM), F32),
                        pltpu.VMEM((tc, HEAD_DIM), F32),
                        pltpu.VMEM((tc, SUB * HEAD_DIM), BF16),
                        pltpu.VMEM((HEAD_DIM, HEAD_DIM), F32),
                        pltpu.VMEM((tc, HEAD_DIM), F32)],
        compiler_params=_params("parallel", "arbitrary"),
        name="hgrn_core",
    )(proj, proj, proj, proj, lb_logits, gnorm.reshape(1, HEAD_DIM), u)


def _sb_kernel(q_ref, k_ref, v_ref, w_ref, o_ref, acc_s, c_s, *, tq, hpb):
    qi = pl.program_id(1)
    w = w_ref[...]
    acc_s[...] = jnp.zeros_like(acc_s)
    c_s[...] = jnp.zeros_like(c_s)

    def walk(k0, masks):
        n_sub = len(masks)
        heads = [slice(hh * HEAD_DIM, (hh + 1) * HEAD_DIM) for hh in range(hpb)]
        z = [_dot_nt(q_ref[:, hc], k_ref[pl.ds(k0, n_sub * tq), hc]) for hc in heads]
        r = []
        for hh in range(hpb):
            r_h = []
            for s in range(n_sub):
                zs = z[hh][:, s * tq:(s + 1) * tq]
                nz = -zs
                lk = jnp.minimum(nz, 0.0) - jnp.log2(1.0 + jnp.exp2(jnp.minimum(zs, nz)))
                if masks[s] is not None:
                    lk = jnp.where(masks[s], lk, 0.0)
                r_h.append(_dot(lk.astype(BF16), w))
            r.append(r_h)
        for hh, hc in enumerate(heads):
            c = c_s[hh]
            parts = [None] * n_sub
            for s in range(n_sub - 1, -1, -1):
                cc = jnp.concatenate([c] * (tq // HEAD_DIM), axis=1)
                a = jnp.exp2(z[hh][:, s * tq:(s + 1) * tq] + r[hh][s] + cc)
                if masks[s] is not None:
                    a = jnp.where(masks[s], a, 0.0)
                parts[s] = a
                c = c + jnp.broadcast_to(r[hh][s][:, 0:1], c.shape)
            a = parts[0] if n_sub == 1 else jnp.concatenate(parts, axis=1)
            acc_s[hh] += _dot(a.astype(BF16), v_ref[pl.ds(k0, n_sub * tq), hc])
            c_s[hh] = c

    row = lax.broadcasted_iota(jnp.int32, (tq, tq), 0)
    col = lax.broadcasted_iota(jnp.int32, (tq, tq), 1)
    causal = col < row

    @pl.when(qi == 0)
    def _():
        walk(0, [causal])

    @pl.when(qi > 0)
    def _():
        walk(pl.multiple_of((qi - 1) * tq, tq), [None, causal])

    n_left = jnp.maximum(qi - 1, 0)

    def cond(carry):
        j, c_max = carry
        return jnp.logical_and(j < n_left, c_max > SB_UNDERFLOW_LOG2)

    def body(carry):
        j, _ = carry
        walk(pl.multiple_of((n_left - 1 - j) * tq, tq), [None])
        return j + 1, jnp.max(c_s[...])

    lax.while_loop(cond, body, (jnp.int32(0), jnp.max(c_s[...])))
    for hh in range(hpb):
        o_ref[:, hh * HEAD_DIM:(hh + 1) * HEAD_DIM] = acc_s[hh].astype(o_ref.dtype)


def sb_core(qkv, tq=256, hpb=4):
    s, d3 = qkv.shape
    d = d3 // 3
    groups = d // (hpb * HEAD_DIM)
    j = jnp.arange(tq)
    w = (j[:, None] >= j[None, :]).astype(BF16)
    kern = functools.partial(_sb_kernel, tq=tq, hpb=hpb)
    wide = hpb * HEAD_DIM
    return pl.pallas_call(
        kern,
        out_shape=jax.ShapeDtypeStruct((s, d), BF16),
        grid=(groups, s // tq),
        in_specs=[pl.BlockSpec((tq, wide), lambda h, i: (i, h)),
                  pl.BlockSpec((s, wide), lambda h, i: (0, groups + h)),
                  pl.BlockSpec((s, wide), lambda h, i: (0, 2 * groups + h)),
                  pl.BlockSpec((tq, tq), lambda h, i: (0, 0))],
        out_specs=pl.BlockSpec((tq, wide), lambda h, i: (i, h)),
        scratch_shapes=[pltpu.VMEM((hpb, tq, HEAD_DIM), F32),
                        pltpu.VMEM((hpb, tq, HEAD_DIM), F32)],
        compiler_params=_params("parallel", "arbitrary"),
        name="sb_core",
    )(qkv, qkv, qkv, w)


def _ffn_half_step(h, norm_w, w_gu, w_down, layer):
    f = w_down.shape[1]
    fp = -(-f // FF_PAD) * FF_PAD
    u = rmsnorm(h, norm_w[layer], BF16)
    a, down = ffn_up(u, w_gu, w_down, layer, fp, tn=FF_TILE)
    return matmul_residual(a, down, h, 0.5, tk=fp // 4)


def _cast_weight(w, layer):
    n = w.shape[2]
    tr = 8
    while 2 * tr * n <= 2 * 1024 * 1024 and w.shape[1] % (2 * tr) == 0:
        tr *= 2
    return cast_rows(w, layer, BF16, tr=tr)


def kernel(x, ffn1_norm, ffn1_w_gu, ffn1_w_down, mix_norm, ffn2_norm, ffn2_w_gu, ffn2_w_down,
           hgrn_w_in, hgrn_lb_logits, hgrn_gnorm, hgrn_w_out, sb_w_in, sb_w_out, final_norm):
    b, s, d = x.shape
    depth = ffn1_norm.shape[0]
    outs = []
    for bi in range(b):
        h = x[bi]
        for i in range(depth):
            h = _ffn_half_step(h, ffn1_norm, ffn1_w_gu, ffn1_w_down, i)
            u = rmsnorm(h, mix_norm[i], BF16)
            j = i // 2
            if i % 2 == 0:
                proj = matmul(u, hgrn_w_in, j, F32)
                o = hgrn_core(proj, hgrn_lb_logits, hgrn_gnorm[j], layer=i)
                h = matmul_residual(o, _cast_weight(hgrn_w_out, j), h, 1.0)
            else:
                qkv = matmul(u, sb_w_in, j, BF16, scaled_cols=d,
                             scale=math.log2(math.e) / math.sqrt(HEAD_DIM))
                o = sb_core(qkv)
                h = matmul_residual(o, _cast_weight(sb_w_out, j), h, 1.0)
            h = _ffn_half_step(h, ffn2_norm, ffn2_w_gu, ffn2_w_down, i)
        outs.append(rmsnorm(h, final_norm, x.dtype))
    return jnp.stack(outs, axis=0)
```
